```python
import math
import jax, jax.numpy as jnp
from jax import lax
import numpy as np

D_MODEL = 1024
BATCH = 4
SEQ = 8192
DEPTH = 1
DEC_BATCH = 4
DEC_SEQ = 4096
PAST_LEN = 128

GRID_W = 64
HEAD_DIM = 64
N_HEADS_A = 8
N_KV_A = 2
N_HEADS_B = 8
N_KV_B = 2
Q_BLOCK = 128
WINDOW = 128
N_BUCKETS = 32
MAX_DISTANCE = 128
ROPE_THETA = 10000.0
AXIS_DIM = HEAD_DIM // 2
D_FF = 2816
EPS = 1e-6
NEG_INF = -1e30
SPLIT_SIZES = (N_HEADS_A * HEAD_DIM, N_KV_A * HEAD_DIM, N_KV_A * HEAD_DIM,
               N_HEADS_B * HEAD_DIM, N_KV_B * HEAD_DIM, N_KV_B * HEAD_DIM,
               D_MODEL, D_MODEL)
IN_COLS = sum(SPLIT_SIZES)

kernel_name = "hybrid_axial_window_encoder"


def rmsnorm(x, g):
    xf = x.astype(jnp.float32)
    y = xf * lax.rsqrt(jnp.mean(xf * xf, axis=-1, keepdims=True) + EPS) * g.astype(jnp.float32)
    return y.astype(x.dtype)


def swiglu_ffn(x, w_in, w_out):
    a, b = jnp.split(x @ w_in, 2, axis=-1)
    return (jax.nn.silu(a) * b) @ w_out


def t5_bucket(rel):
    nb = N_BUCKETS // 2
    max_exact = nb // 2
    ret = jnp.where(rel > 0, nb, 0)
    n = jnp.abs(rel)
    large = max_exact + (jnp.log(jnp.maximum(n, 1).astype(jnp.float32) / max_exact)
                         / math.log(MAX_DISTANCE / max_exact) * (nb - max_exact)).astype(jnp.int32)
    large = jnp.minimum(large, nb - 1)
    return ret + jnp.where(n < max_exact, n, large)


def axial_rope(x):
    B, T, H, _ = x.shape
    rows = T // GRID_W
    grid_r, grid_c = jnp.meshgrid(jnp.arange(rows, dtype=jnp.float32),
                                  jnp.arange(GRID_W, dtype=jnp.float32), indexing="ij")
    pos = jnp.stack([grid_r.reshape(-1), grid_c.reshape(-1)], axis=-1)
    inv = ROPE_THETA ** (-jnp.arange(0, AXIS_DIM, 2, dtype=jnp.float32) / AXIS_DIM)
    ang = pos[:, :, None] * inv
    cos = jnp.cos(ang)[None, :, None]
    sin = jnp.sin(ang)[None, :, None]
    xf = x.astype(jnp.float32).reshape(B, T, H, 2, 2, AXIS_DIM // 2)
    x1 = xf[..., 0, :]
    x2 = xf[..., 1, :]
    out = jnp.stack([x1 * cos - x2 * sin, x2 * cos + x1 * sin], axis=-2)
    return out.reshape(x.shape).astype(x.dtype)


def dense_attention_blocks(q, k, v):
    B, T, KV, G, D = q.shape
    nb = T // Q_BLOCK
    scale = 1.0 / math.sqrt(D)
    kf = k.astype(jnp.float32)
    vf = v.astype(jnp.float32)
    qb = q.reshape(B, nb, Q_BLOCK, KV, G, D).transpose(1, 0, 2, 3, 4, 5)

    def one_block(qblk):
        s = jnp.einsum("bqkgd,bskd->bkgqs", qblk.astype(jnp.float32), kf) * scale
        p = jax.nn.softmax(s, axis=-1)
        return jnp.einsum("bkgqs,bskd->bqkgd", p, vf)

    o = lax.map(one_block, qb)
    return o.transpose(1, 0, 2, 3, 4, 5).reshape(B, T, KV * G * D).astype(q.dtype)


def window_sink_attention(q, k, v, rel_bias, sink):
    B, T, KV, G, D = q.shape
    nb = T // Q_BLOCK
    scale = 1.0 / math.sqrt(D)
    pad = ((0, 0), (Q_BLOCK, Q_BLOCK), (0, 0), (0, 0))
    kp = jnp.pad(k.astype(jnp.float32), pad).reshape(B, nb + 2, Q_BLOCK, KV, D)
    vp = jnp.pad(v.astype(jnp.float32), pad).reshape(B, nb + 2, Q_BLOCK, KV, D)
    k_band = jnp.concatenate([kp[:, :-2], kp[:, 1:-1], kp[:, 2:]], axis=2)
    v_band = jnp.concatenate([vp[:, :-2], vp[:, 1:-1], vp[:, 2:]], axis=2)
    qb = q.astype(jnp.float32).reshape(B, nb, Q_BLOCK, KV, G, D)
    rel = jnp.arange(3 * Q_BLOCK)[None, :] - Q_BLOCK - jnp.arange(Q_BLOCK)[:, None]
    bias = rel_bias.astype(jnp.float32)[t5_bucket(rel)]
    bias = bias.transpose(2, 0, 1).reshape(KV, G, Q_BLOCK, 3 * Q_BLOCK)
    key_pos = jnp.arange(nb)[:, None] * Q_BLOCK - Q_BLOCK + jnp.arange(3 * Q_BLOCK)[None, :]
    in_range = (key_pos >= 0) & (key_pos < T)
    mask = (jnp.abs(rel) <= WINDOW)[None] & in_range[:, None, :]
    s = jnp.einsum("bnqkgd,bnskd->bnkgqs", qb, k_band) * scale + bias[None, None]
    s = jnp.where(mask[None, :, None, None], s, NEG_INF)
    sink_col = jnp.broadcast_to(sink.astype(jnp.float32).reshape(1, 1, KV, G, 1, 1),
                                s.shape[:-1] + (1,))
    p = jax.nn.softmax(jnp.concatenate([s, sink_col], axis=-1), axis=-1)[..., :-1]
    o = jnp.einsum("bnkgqs,bnskd->bnqkgd", p, v_band)
    return o.reshape(B, T, KV * G * D).astype(q.dtype)


def token_mixing(h, w_in, q_norm_a, k_norm_a, sink_b, w_branch_a, w_branch_b, w_out, rel_bias):
    B, T, _ = h.shape
    proj = h @ w_in
    offsets = list(np.cumsum(SPLIT_SIZES)[:-1])
    qa, ka, va, qb, kb, vb, ga, gb = jnp.split(proj, offsets, axis=-1)
    qa = axial_rope(rmsnorm(qa.reshape(B, T, N_HEADS_A, HEAD_DIM), q_norm_a))
    ka = axial_rope(rmsnorm(ka.reshape(B, T, N_KV_A, HEAD_DIM), k_norm_a))
    qa = qa.reshape(B, T, N_KV_A, N_HEADS_A // N_KV_A, HEAD_DIM)
    va = va.reshape(B, T, N_KV_A, HEAD_DIM)
    ya = dense_attention_blocks(qa, ka, va) @ w_branch_a
    qb = qb.reshape(B, T, N_KV_B, N_HEADS_B // N_KV_B, HEAD_DIM)
    kb = kb.reshape(B, T, N_KV_B, HEAD_DIM)
    vb = vb.reshape(B, T, N_KV_B, HEAD_DIM)
    yb = window_sink_attention(qb, kb, vb, rel_bias, sink_b) @ w_branch_b
    merged = jax.nn.sigmoid(ga) * ya + jax.nn.sigmoid(gb) * yb
    return merged @ w_out


def trunk(x, norm_ffn1, w_ffn1_in, w_ffn1_out, norm_mix, w_in, q_norm_a, k_norm_a, sink_b,
          w_branch_a, w_branch_b, w_out, norm_ffn2, w_ffn2_in, w_ffn2_out, rel_bias, norm_final):
    for l in range(DEPTH):
        x = x + 0.5 * swiglu_ffn(rmsnorm(x, norm_ffn1[l]), w_ffn1_in[l], w_ffn1_out[l])
        x = x + token_mixing(rmsnorm(x, norm_mix[l]), w_in[l], q_norm_a[l], k_norm_a[l], sink_b[l],
                             w_branch_a[l], w_branch_b[l], w_out[l], rel_bias)
        x = x + 0.5 * swiglu_ffn(rmsnorm(x, norm_ffn2[l]), w_ffn2_in[l], w_ffn2_out[l])
    return rmsnorm(x, norm_final)


def setup_inputs(seed: int = 0) -> dict:
    key = jax.random.key(seed)
    ks = jax.random.split(key, 20)
    f32 = jnp.float32

    def nrm(k, shape, scale):
        return jax.random.normal(k, shape, f32) * scale

    def gain(k, shape):
        return 1.0 + 0.05 * jax.random.normal(k, shape, f32)

    wa = N_HEADS_A * HEAD_DIM
    wb = N_HEADS_B * HEAD_DIM
    return {
        "x_prompt": jax.random.normal(ks[0], (BATCH, SEQ, D_MODEL), f32),
        "x_sample": jax.random.normal(ks[1], (DEC_BATCH, DEC_SEQ, D_MODEL), f32),
        "norm_ffn1": gain(ks[2], (DEPTH, D_MODEL)),
        "w_ffn1_in": nrm(ks[3], (DEPTH, D_MODEL, 2 * D_FF), D_MODEL ** -0.5),
        "w_ffn1_out": nrm(ks[4], (DEPTH, D_FF, D_MODEL), D_FF ** -0.5),
        "norm_mix": gain(ks[5], (DEPTH, D_MODEL)),
        "w_in": nrm(ks[6], (DEPTH, D_MODEL, IN_COLS), D_MODEL ** -0.5),
        "q_norm_a": gain(ks[7], (DEPTH, HEAD_DIM)),
        "k_norm_a": gain(ks[8], (DEPTH, HEAD_DIM)),
        "sink_b": nrm(ks[9], (DEPTH, N_HEADS_B), 0.5),
        "w_branch_a": nrm(ks[10], (DEPTH, wa, D_MODEL), wa ** -0.5),
        "w_branch_b": nrm(ks[11], (DEPTH, wb, D_MODEL), wb ** -0.5),
        "w_out": nrm(ks[12], (DEPTH, D_MODEL, D_MODEL), D_MODEL ** -0.5),
        "norm_ffn2": gain(ks[13], (DEPTH, D_MODEL)),
        "w_ffn2_in": nrm(ks[14], (DEPTH, D_MODEL, 2 * D_FF), D_MODEL ** -0.5),
        "w_ffn2_out": nrm(ks[15], (DEPTH, D_FF, D_MODEL), D_FF ** -0.5),
        "rel_bias": nrm(ks[16], (N_BUCKETS, N_HEADS_B), 0.1),
        "norm_final": gain(ks[17], (D_MODEL,)),
    }


def reference(x_prompt, x_sample, norm_ffn1, w_ffn1_in, w_ffn1_out, norm_mix, w_in, q_norm_a, k_norm_a,
              sink_b, w_branch_a, w_branch_b, w_out, norm_ffn2, w_ffn2_in, w_ffn2_out, rel_bias, norm_final):
    y_prompt = trunk(x_prompt, norm_ffn1, w_ffn1_in, w_ffn1_out, norm_mix, w_in, q_norm_a, k_norm_a, sink_b,
                     w_branch_a, w_branch_b, w_out, norm_ffn2, w_ffn2_in, w_ffn2_out, rel_bias, norm_final)
    y_sample = trunk(x_sample, norm_ffn1, w_ffn1_in, w_ffn1_out, norm_mix, w_in, q_norm_a, k_norm_a, sink_b,
                     w_branch_a, w_branch_b, w_out, norm_ffn2, w_ffn2_in, w_ffn2_out, rel_bias, norm_final)
    return (y_prompt, y_sample)
```

```python
import functools
import math

import jax
import jax.numpy as jnp
from jax import lax
from jax.experimental import pallas as pl
from jax.experimental.pallas import tpu as pltpu

D_MODEL = 1024
GRID_W = 64
HEAD_DIM = 64
N_HEADS = 8
N_KV = 2
GROUP = N_HEADS // N_KV
Q_BLOCK = 128
WINDOW = 128
N_BUCKETS = 32
MAX_DISTANCE = 128
ROPE_THETA = 10000.0
AXIS_DIM = HEAD_DIM // 2
ROT = AXIS_DIM // 2
D_FF = 2816
EPS = 1e-6
NEG_INF = -1e30
Q_COLS = N_HEADS * HEAD_DIM
KV_COLS = N_KV * HEAD_DIM

F32 = jnp.float32
BF16 = jnp.bfloat16

V7X_LANES = 128
V7X_VMEM_BYTES = 64 * 1024 * 1024

TOKEN_TILE = 256
KEY_TILE_A = 256
Q_TILE_A = 128
VT_CHUNK = 128

FM_ROWS = Q_COLS + KV_COLS + KV_COLS + Q_COLS + KV_COLS
TM_COLS = KV_COLS + 2 * D_MODEL


def _resident(shape):
    nd = len(shape)
    return pl.BlockSpec(shape, lambda *_: (0,) * nd, pipeline_mode=pl.Buffered(1))


def _rms(x, g):
    ms = jnp.mean(x * x, axis=-1, keepdims=True)
    return x * lax.rsqrt(ms + EPS) * g


def _sigmoid(x):
    return 1.0 / (1.0 + jnp.exp(-x))


def _swiglu_half_step(x, g_ref, w_in_ref, w_out_ref):
    h = _rms(x, g_ref[...]).astype(BF16)
    ab = jnp.dot(h, w_in_ref[...], preferred_element_type=F32)
    a = ab[:, :D_FF]
    b = ab[:, D_FF:]
    act = (a * _sigmoid(a) * b).astype(BF16)
    return x + 0.5 * jnp.dot(act, w_out_ref[...], preferred_element_type=F32)


def _head_norm_rope_t(z, gain, cs):
    ms = jnp.mean(z * z, axis=0, keepdims=True)
    z = z * lax.rsqrt(ms + EPS) * gain
    cr, sr, cc, sc = cs[0:ROT], cs[ROT:2 * ROT], cs[2 * ROT:3 * ROT], cs[3 * ROT:4 * ROT]
    x1r, x2r = z[0:ROT], z[ROT:2 * ROT]
    x1c, x2c = z[2 * ROT:3 * ROT], z[3 * ROT:4 * ROT]
    return jnp.concatenate(
        [x1r * cr - x2r * sr, x2r * cr + x1r * sr, x1c * cc - x2c * sc, x2c * cc + x1c * sc], axis=0)


def _front_kernel(x_ref, n1_ref, w1i_ref, w1o_ref, nm_ref, wfm_ref, wtm_ref, qn_ref, kn_ref, cs_ref,
                  x1_ref, qa_ref, ka_ref, va_ref, qb_ref, kb_ref, vb_ref, g_ref):
    tm = x_ref.shape[0]
    x1 = _swiglu_half_step(x_ref[...], n1_ref, w1i_ref, w1o_ref)
    x1_ref[...] = x1
    h = _rms(x1, nm_ref[...]).astype(BF16)
    pt = lax.dot_general(wfm_ref[...], h, (((1,), (1,)), ((), ())), preferred_element_type=F32)
    ptm = jnp.dot(h, wtm_ref[...], preferred_element_type=F32)
    cs = cs_ref[...]
    qn = qn_ref[...]
    kn = kn_ref[...]
    scale = 1.0 / math.sqrt(HEAD_DIM)
    for hd in range(N_HEADS):
        z = pt[hd * HEAD_DIM:(hd + 1) * HEAD_DIM]
        qa_ref[0, hd * HEAD_DIM:(hd + 1) * HEAD_DIM, :] = (_head_norm_rope_t(z, qn, cs) * scale).astype(BF16)
    ka_t = jnp.concatenate(
        [_head_norm_rope_t(pt[Q_COLS + kv * HEAD_DIM:Q_COLS + (kv + 1) * HEAD_DIM], kn, cs) for kv in range(N_KV)],
        axis=0)
    ka_ref[0] = ka_t.T.astype(BF16)
    off = Q_COLS + KV_COLS
    va = pt[off:off + KV_COLS].astype(BF16)
    off += KV_COLS
    qb_ref[0] = (pt[off:off + Q_COLS] * scale).astype(BF16)
    off += Q_COLS
    vb = pt[off:off + KV_COLS].astype(BF16)
    for c in range(tm // VT_CHUNK):
        va_ref[0, c] = va[:, c * VT_CHUNK:(c + 1) * VT_CHUNK]
        vb_ref[0, c] = vb[:, c * VT_CHUNK:(c + 1) * VT_CHUNK]
    kb_ref[0] = ptm[:, :KV_COLS].astype(BF16)
    g_ref[...] = ptm[:, KV_COLS:].astype(BF16)


def _front(x, p, cs, batch, seq):
    tm = TOKEN_TILE
    n = batch * seq
    nt = seq // tm
    grid = (n // tm,)
    tok = lambda i: (i, 0)
    bt3 = lambda i: (i // nt, 0, i % nt)
    out_shape = (
        jax.ShapeDtypeStruct((n, D_MODEL), F32),
        jax.ShapeDtypeStruct((batch, Q_COLS, seq), BF16),
        jax.ShapeDtypeStruct((batch, seq, KV_COLS), BF16),
        jax.ShapeDtypeStruct((batch, seq // VT_CHUNK, KV_COLS, VT_CHUNK), BF16),
        jax.ShapeDtypeStruct((batch, Q_COLS, seq), BF16),
        jax.ShapeDtypeStruct((batch, seq, KV_COLS), BF16),
        jax.ShapeDtypeStruct((batch, seq // VT_CHUNK, KV_COLS, VT_CHUNK), BF16),
        jax.ShapeDtypeStruct((n, 2 * D_MODEL), BF16),
    )
    vt_spec = pl.BlockSpec((1, tm // VT_CHUNK, KV_COLS, VT_CHUNK), lambda i: (i // nt, i % nt, 0, 0))
    out_specs = (
        pl.BlockSpec((tm, D_MODEL), tok),
        pl.BlockSpec((1, Q_COLS, tm), bt3),
        pl.BlockSpec((1, tm, KV_COLS), lambda i: (i // nt, i % nt, 0)),
        vt_spec,
        pl.BlockSpec((1, Q_COLS, tm), bt3),
        pl.BlockSpec((1, tm, KV_COLS), lambda i: (i // nt, i % nt, 0)),
        vt_spec,
        pl.BlockSpec((tm, 2 * D_MODEL), tok),
    )
    in_specs = [
        pl.BlockSpec((tm, D_MODEL), tok),
        _resident((1, D_MODEL)),
        _resident((D_MODEL, 2 * D_FF)),
        _resident((D_FF, D_MODEL)),
        _resident((1, D_MODEL)),
        _resident((FM_ROWS, D_MODEL)),
        _resident((D_MODEL, TM_COLS)),
        _resident((HEAD_DIM, 1)),
        _resident((HEAD_DIM, 1)),
        pl.BlockSpec((4 * ROT, tm), lambda i: (0, i % nt)),
    ]
    return pl.pallas_call(
        _front_kernel,
        grid=grid,
        in_specs=in_specs,
        out_specs=out_specs,
        out_shape=out_shape,
        compiler_params=pltpu.CompilerParams(
            dimension_semantics=("arbitrary",), vmem_limit_bytes=56 * 1024 * 1024),
        name="front",
    )(x, p["n1"], p["w1i"], p["w1o"], p["nm"], p["wfm"], p["wtm"], p["qn"], p["kn"], cs)


def _build_q_weights(q, wq_ref, tq):
    zeros = jnp.zeros((HEAD_DIM, tq), BF16)
    for hd in range(N_HEADS):
        qh = q[hd * HEAD_DIM:(hd + 1) * HEAD_DIM]
        blk = jnp.concatenate([qh, zeros], axis=0) if hd < GROUP else jnp.concatenate([zeros, qh], axis=0)
        wq_ref[:, hd * tq:(hd + 1) * tq] = blk


def _attn_a_kernel(q_ref, k_ref, v_ref, o_ref, wq_ref, m_ref, l_ref, acc_ref, *, seq):
    tq = q_ref.shape[2]
    half = GROUP * tq
    tk = KEY_TILE_A
    cpt = tk // VT_CHUNK
    _build_q_weights(q_ref[0], wq_ref, tq)
    m_ref[...] = jnp.full(m_ref.shape, NEG_INF, F32)
    l_ref[...] = jnp.zeros(l_ref.shape, F32)
    acc_ref[...] = jnp.zeros(acc_ref.shape, F32)

    def body(j, carry):
        kt = k_ref[0, pl.ds(pl.multiple_of(j * tk, tk), tk), :]
        s = jnp.dot(kt, wq_ref[...], preferred_element_type=F32)
        m_old = m_ref[...]
        m_new = jnp.maximum(m_old, jnp.max(s, axis=0, keepdims=True))
        alpha = jnp.exp(m_old - m_new)
        p = jnp.exp(s - m_new)
        l_ref[...] = alpha * l_ref[...] + jnp.sum(p, axis=0, keepdims=True)
        m_ref[...] = m_new
        pb = p.astype(BF16)
        vt = jnp.concatenate([v_ref[0, j * cpt + c] for c in range(cpt)], axis=1)
        for kv in range(N_KV):
            cols = slice(kv * half, (kv + 1) * half)
            pv = jnp.dot(vt[kv * HEAD_DIM:(kv + 1) * HEAD_DIM], pb[:, cols], preferred_element_type=F32)
            acc_ref[kv] = acc_ref[kv] * alpha[:, cols] + pv
        return carry

    lax.fori_loop(0, seq // tk, body, 0)
    inv_l = 1.0 / l_ref[...]
    rows = []
    for hd in range(N_HEADS):
        kv, g = divmod(hd, GROUP)
        cols = slice(g * tq, (g + 1) * tq)
        rows.append(acc_ref[kv][:, cols] * inv_l[:, hd * tq:(hd + 1) * tq])
    o_ref[0] = jnp.concatenate(rows, axis=0).T.astype(BF16)


def _attn_a(qt, k, vt, batch, seq):
    tq = Q_TILE_A
    kernel = functools.partial(_attn_a_kernel, seq=seq)
    return pl.pallas_call(
        kernel,
        grid=(batch, seq // tq),
        in_specs=[
            pl.BlockSpec((1, Q_COLS, tq), lambda b, i: (b, 0, i)),
            pl.BlockSpec((1, seq, KV_COLS), lambda b, i: (b, 0, 0)),
            pl.BlockSpec((1, seq // VT_CHUNK, KV_COLS, VT_CHUNK), lambda b, i: (b, 0, 0, 0)),
        ],
        out_specs=pl.BlockSpec((1, tq, Q_COLS), lambda b, i: (b, i, 0)),
        out_shape=jax.ShapeDtypeStruct((batch, seq, Q_COLS), BF16),
        scratch_shapes=[
            pltpu.VMEM((KV_COLS, N_HEADS * tq), BF16),
            pltpu.VMEM((1, N_HEADS * tq), F32),
            pltpu.VMEM((1, N_HEADS * tq), F32),
            pltpu.VMEM((N_KV, HEAD_DIM, GROUP * tq), F32),
        ],
        compiler_params=pltpu.CompilerParams(
            dimension_semantics=("arbitrary", "arbitrary"), vmem_limit_bytes=40 * 1024 * 1024),
        name="attn_a",
    )(qt, k, vt)


def _attn_b_kernel(rb_ref, q_ref, k_ref, v_ref, bucket_ref, sink_ref, o_ref, wq_ref, bias_ref, *, seq):
    tq = Q_BLOCK
    nb = seq // tq
    half = GROUP * tq
    i = pl.program_id(1)

    @pl.when((pl.program_id(0) == 0) & (i == 0))
    def _():
        for jb in range(3):
            bucket = bucket_ref[jb]
            key = lax.broadcasted_iota(jnp.int32, (tq, tq), 0)
            qry = lax.broadcasted_iota(jnp.int32, (tq, tq), 1)
            rel = (jb - 1) * tq + key - qry
            inside = jnp.abs(rel) <= WINDOW
            for hd in range(N_HEADS):
                tbl = jnp.zeros((tq, tq), F32)
                for b in range(N_BUCKETS):
                    tbl = jnp.where(bucket == b, rb_ref[b, hd], tbl)
                bias_ref[jb, :, hd * tq:(hd + 1) * tq] = jnp.where(inside, tbl, NEG_INF)

    _build_q_weights(q_ref[0], wq_ref, tq)
    parts = []
    idxs = []
    for jb in range(3):
        blk = i - 1 + jb
        valid = (blk >= 0) & (blk < nb)
        idx = jnp.clip(blk, 0, nb - 1)
        kt = k_ref[0, pl.ds(pl.multiple_of(idx * tq, tq), tq), :]
        s = jnp.dot(kt, wq_ref[...], preferred_element_type=F32) + bias_ref[jb]
        parts.append(jnp.where(valid, s, NEG_INF))
        idxs.append(idx)
    s = jnp.concatenate(parts, axis=0)
    sink = sink_ref[...]
    m = jnp.maximum(jnp.max(s, axis=0, keepdims=True), sink)
    p = jnp.exp(s - m)
    inv_l = 1.0 / (jnp.sum(p, axis=0, keepdims=True) + jnp.exp(sink - m))
    pb = p.astype(BF16)
    rows = []
    for kv in range(N_KV):
        acc = jnp.zeros((HEAD_DIM, half), F32)
        for jb in range(3):
            vt = v_ref[0, idxs[jb]][kv * HEAD_DIM:(kv + 1) * HEAD_DIM]
            acc = acc + jnp.dot(vt, pb[jb * tq:(jb + 1) * tq, kv * half:(kv + 1) * half],
                                preferred_element_type=F32)
        acc = acc * inv_l[:, kv * half:(kv + 1) * half]
        for g in range(GROUP):
            rows.append(acc[:, g * tq:(g + 1) * tq])
    o_ref[0] = jnp.concatenate(rows, axis=0).T.astype(BF16)


def _attn_b(qt, k, vt, rel_bias, bucket_t, sink_row, batch, seq):
    tq = Q_BLOCK
    kernel = functools.partial(_attn_b_kernel, seq=seq)
    return pl.pallas_call(
        kernel,
        grid=(batch, seq // tq),
        in_specs=[
            pl.BlockSpec(memory_space=pltpu.SMEM),
            pl.BlockSpec((1, Q_COLS, tq), lambda b, i: (b, 0, i)),
            pl.BlockSpec((1, seq, KV_COLS), lambda b, i: (b, 0, 0)),
            pl.BlockSpec((1, seq // VT_CHUNK, KV_COLS, VT_CHUNK), lambda b, i: (b, 0, 0, 0)),
            pl.BlockSpec((3, tq, tq), lambda b, i: (0, 0, 0)),
            pl.BlockSpec((1, N_HEADS * tq), lambda b, i: (0, 0)),
        ],
        out_specs=pl.BlockSpec((1, tq, Q_COLS), lambda b, i: (b, i, 0)),
        out_shape=jax.ShapeDtypeStruct((batch, seq, Q_COLS), BF16),
        scratch_shapes=[
            pltpu.VMEM((KV_COLS, N_HEADS * tq), BF16),
            pltpu.VMEM((3, tq, N_HEADS * tq), F32),
        ],
        compiler_params=pltpu.CompilerParams(
            dimension_semantics=("arbitrary", "arbitrary"), vmem_limit_bytes=40 * 1024 * 1024),
        name="attn_b",
    )(rel_bias, qt, k, vt, bucket_t, sink_row)


def _back_kernel(x1_ref, oa_ref, ob_ref, g_ref, wba_ref, wbb_ref, wo_ref, n2_ref, w2i_ref, w2o_ref, nf_ref, y_ref):
    ya = jnp.dot(oa_ref[...], wba_ref[...], preferred_element_type=F32)
    yb = jnp.dot(ob_ref[...], wbb_ref[...], preferred_element_type=F32)
    g = g_ref[...].astype(F32)
    merged = _sigmoid(g[:, :D_MODEL]) * ya + _sigmoid(g[:, D_MODEL:]) * yb
    x2 = x1_ref[...] + jnp.dot(merged.astype(BF16), wo_ref[...], preferred_element_type=F32)
    x3 = _swiglu_half_step(x2, n2_ref, w2i_ref, w2o_ref)
    y_ref[...] = _rms(x3, nf_ref[...])


def _back(x1, oa, ob, g, p):
    tm = TOKEN_TILE
    n = x1.shape[0]
    tok = lambda i: (i, 0)
    return pl.pallas_call(
        _back_kernel,
        grid=(n // tm,),
        in_specs=[
            pl.BlockSpec((tm, D_MODEL), tok),
            pl.BlockSpec((tm, Q_COLS), tok),
            pl.BlockSpec((tm, Q_COLS), tok),
            pl.BlockSpec((tm, 2 * D_MODEL), tok),
            _resident((Q_COLS, D_MODEL)),
            _resident((Q_COLS, D_MODEL)),
            _resident((D_MODEL, D_MODEL)),
            _resident((1, D_MODEL)),
            _resident((D_MODEL, 2 * D_FF)),
            _resident((D_FF, D_MODEL)),
            _resident((1, D_MODEL)),
        ],
        out_specs=pl.BlockSpec((tm, D_MODEL), tok),
        out_shape=jax.ShapeDtypeStruct((n, D_MODEL), F32),
        compiler_params=pltpu.CompilerParams(
            dimension_semantics=("arbitrary",), vmem_limit_bytes=56 * 1024 * 1024),
        name="back",
    )(x1, oa, ob, g, p["wba"], p["wbb"], p["wo"], p["n2"], p["w2i"], p["w2o"], p["nf"])


def _t5_bucket(rel):
    nb = N_BUCKETS // 2
    max_exact = nb // 2
    ret = jnp.where(rel > 0, nb, 0)
    n = jnp.abs(rel)
    large = max_exact + (jnp.log(jnp.maximum(n, 1).astype(F32) / max_exact)
                         / math.log(MAX_DISTANCE / max_exact) * (nb - max_exact)).astype(jnp.int32)
    large = jnp.minimum(large, nb - 1)
    return ret + jnp.where(n < max_exact, n, large)


def _rope_table(seq):
    rows = seq // GRID_W
    grid_r, grid_c = jnp.meshgrid(jnp.arange(rows, dtype=F32), jnp.arange(GRID_W, dtype=F32), indexing="ij")
    pos = jnp.stack([grid_r.reshape(-1), grid_c.reshape(-1)], axis=-1)
    inv = ROPE_THETA ** (-jnp.arange(0, AXIS_DIM, 2, dtype=F32) / AXIS_DIM)
    ang = pos[:, :, None] * inv
    cos = jnp.cos(ang)
    sin = jnp.sin(ang)
    return jnp.concatenate([cos[:, 0].T, sin[:, 0].T, cos[:, 1].T, sin[:, 1].T], axis=0)


def _trunk(x, p, bucket_t, sink_row, rel_bias):
    batch, seq, _ = x.shape
    cs = _rope_table(seq)
    x1, qa, ka, va, qb, kb, vb, g = _front(x.reshape(batch * seq, D_MODEL), p, cs, batch, seq)
    oa = _attn_a(qa, ka, va, batch, seq)
    ob = _attn_b(qb, kb, vb, rel_bias, bucket_t, sink_row, batch, seq)
    y = _back(x1, oa.reshape(batch * seq, Q_COLS), ob.reshape(batch * seq, Q_COLS), g, p)
    return y.reshape(batch, seq, D_MODEL)


def kernel(x_prompt, x_sample, norm_ffn1, w_ffn1_in, w_ffn1_out, norm_mix, w_in, q_norm_a, k_norm_a, sink_b,
           w_branch_a, w_branch_b, w_out, norm_ffn2, w_ffn2_in, w_ffn2_out, rel_bias, norm_final):
    assert norm_ffn1.shape[0] == 1, "single-layer trunk"
    wi = w_in[0]
    c = 0
    parts = {}
    for name, width in (("qa", Q_COLS), ("ka", KV_COLS), ("va", KV_COLS), ("qb", Q_COLS), ("kb", KV_COLS),
                        ("vb", KV_COLS), ("ga", D_MODEL), ("gb", D_MODEL)):
        parts[name] = wi[:, c:c + width]
        c += width
    wfm = jnp.concatenate([parts[k] for k in ("qa", "ka", "va", "qb", "vb")], axis=1).T.astype(BF16)
    wtm = jnp.concatenate([parts[k] for k in ("kb", "ga", "gb")], axis=1).astype(BF16)
    p = dict(
        n1=norm_ffn1[0].reshape(1, D_MODEL), w1i=w_ffn1_in[0].astype(BF16), w1o=w_ffn1_out[0].astype(BF16),
        nm=norm_mix[0].reshape(1, D_MODEL), wfm=wfm, wtm=wtm,
        qn=q_norm_a[0].reshape(HEAD_DIM, 1), kn=k_norm_a[0].reshape(HEAD_DIM, 1),
        wba=w_branch_a[0].astype(BF16), wbb=w_branch_b[0].astype(BF16), wo=w_out[0].astype(BF16),
        n2=norm_ffn2[0].reshape(1, D_MODEL), w2i=w_ffn2_in[0].astype(BF16), w2o=w_ffn2_out[0].astype(BF16),
        nf=norm_final.reshape(1, D_MODEL),
    )
    key = jnp.arange(3 * Q_BLOCK)[:, None] - Q_BLOCK
    qry = jnp.arange(Q_BLOCK)[None, :]
    bucket_t = _t5_bucket(key - qry).astype(jnp.int32).reshape(3, Q_BLOCK, Q_BLOCK)
    sink_row = jnp.repeat(sink_b[0].astype(F32), Q_BLOCK).reshape(1, N_HEADS * Q_BLOCK)
    rb = rel_bias.astype(F32)
    y_prompt = _trunk(x_prompt, p, bucket_t, sink_row, rb)
    y_sample = _trunk(x_sample, p, bucket_t, sink_row, rb)
    return (y_prompt, y_sample)
```

```python
import functools
import math

import jax
import jax.numpy as jnp
from jax import lax
from jax.experimental import pallas as pl
from jax.experimental.pallas import tpu as pltpu

D_MODEL = 1024
GRID_W = 64
HEAD_DIM = 64
N_HEADS = 8
N_KV = 2
GROUP = N_HEADS // N_KV
Q_BLOCK = 128
WINDOW = 128
N_BUCKETS = 32
MAX_DISTANCE = 128
ROPE_THETA = 10000.0
AXIS_DIM = HEAD_DIM // 2
ROT = AXIS_DIM // 2
D_FF = 2816
EPS = 1e-6
NEG_INF = -1e30
Q_COLS = N_HEADS * HEAD_DIM
KV_COLS = N_KV * HEAD_DIM

F32 = jnp.float32
BF16 = jnp.bfloat16

V7X_LANES = 128
V7X_VMEM_BYTES = 64 * 1024 * 1024

TOKEN_TILE = 256
KEY_TILE_A = 256
Q_TILE_A = 128
VT_CHUNK = 128

FM_ROWS = Q_COLS + KV_COLS + KV_COLS + Q_COLS + KV_COLS
TM_COLS = KV_COLS + 2 * D_MODEL


def _resident(shape):
    nd = len(shape)
    return pl.BlockSpec(shape, lambda *_: (0,) * nd, pipeline_mode=pl.Buffered(1))


def _rms(x, g):
    ms = jnp.mean(x * x, axis=-1, keepdims=True)
    return x * lax.rsqrt(ms + EPS) * g


def _sigmoid(x):
    return 1.0 / (1.0 + jnp.exp(-x))


def _swiglu_half_step(x, g_ref, w_in_ref, w_out_ref):
    h = _rms(x, g_ref[...]).astype(BF16)
    ab = jnp.dot(h, w_in_ref[...], preferred_element_type=F32)
    a = ab[:, :D_FF]
    b = ab[:, D_FF:]
    act = (a * _sigmoid(a) * b).astype(BF16)
    return x + 0.5 * jnp.dot(act, w_out_ref[...], preferred_element_type=F32)


def _head_norm_rope_t(z, gain, cs):
    ms = jnp.mean(z * z, axis=0, keepdims=True)
    z = z * lax.rsqrt(ms + EPS) * gain
    cr, sr, cc, sc = cs[0:ROT], cs[ROT:2 * ROT], cs[2 * ROT:3 * ROT], cs[3 * ROT:4 * ROT]
    x1r, x2r = z[0:ROT], z[ROT:2 * ROT]
    x1c, x2c = z[2 * ROT:3 * ROT], z[3 * ROT:4 * ROT]
    return jnp.concatenate(
        [x1r * cr - x2r * sr, x2r * cr + x1r * sr, x1c * cc - x2c * sc, x2c * cc + x1c * sc], axis=0)


def _front_kernel(x_ref, n1_ref, w1i_ref, w1o_ref, nm_ref, wfm_ref, wtm_ref, qn_ref, kn_ref, cs_ref,
                  x1_ref, qa_ref, ka_ref, va_ref, qb_ref, kb_ref, vb_ref, g_ref):
    tm = x_ref.shape[0]
    x1 = _swiglu_half_step(x_ref[...], n1_ref, w1i_ref, w1o_ref)
    x1_ref[...] = x1
    h = _rms(x1, nm_ref[...]).astype(BF16)
    pt = lax.dot_general(wfm_ref[...], h, (((1,), (1,)), ((), ())), preferred_element_type=F32)
    ptm = jnp.dot(h, wtm_ref[...], preferred_element_type=F32)
    cs = cs_ref[...]
    qn = qn_ref[...]
    kn = kn_ref[...]
    scale = 1.0 / math.sqrt(HEAD_DIM)
    scale_a = scale * math.log2(math.e)
    for hd in range(N_HEADS):
        z = pt[hd * HEAD_DIM:(hd + 1) * HEAD_DIM]
        qa_ref[0, hd * HEAD_DIM:(hd + 1) * HEAD_DIM, :] = (_head_norm_rope_t(z, qn, cs) * scale_a).astype(BF16)
    ka_t = jnp.concatenate(
        [_head_norm_rope_t(pt[Q_COLS + kv * HEAD_DIM:Q_COLS + (kv + 1) * HEAD_DIM], kn, cs) for kv in range(N_KV)],
        axis=0)
    ka_ref[0] = ka_t.T.astype(BF16)
    off = Q_COLS + KV_COLS
    va = pt[off:off + KV_COLS].astype(BF16)
    off += KV_COLS
    qb_ref[0] = (pt[off:off + Q_COLS] * scale).astype(BF16)
    off += Q_COLS
    vb = pt[off:off + KV_COLS].astype(BF16)
    for c in range(tm // VT_CHUNK):
        va_ref[0, c] = va[:, c * VT_CHUNK:(c + 1) * VT_CHUNK]
        vb_ref[0, c] = vb[:, c * VT_CHUNK:(c + 1) * VT_CHUNK]
    kb_ref[0] = ptm[:, :KV_COLS].astype(BF16)
    g_ref[...] = ptm[:, KV_COLS:].astype(BF16)


def _front(x, p, cs, batch, seq):
    tm = TOKEN_TILE
    n = batch * seq
    nt = seq // tm
    grid = (n // tm,)
    tok = lambda i: (i, 0)
    bt3 = lambda i: (i // nt, 0, i % nt)
    out_shape = (
        jax.ShapeDtypeStruct((n, D_MODEL), F32),
        jax.ShapeDtypeStruct((batch, Q_COLS, seq), BF16),
        jax.ShapeDtypeStruct((batch, seq, KV_COLS), BF16),
        jax.ShapeDtypeStruct((batch, seq // VT_CHUNK, KV_COLS, VT_CHUNK), BF16),
        jax.ShapeDtypeStruct((batch, Q_COLS, seq), BF16),
        jax.ShapeDtypeStruct((batch, seq, KV_COLS), BF16),
        jax.ShapeDtypeStruct((batch, seq // VT_CHUNK, KV_COLS, VT_CHUNK), BF16),
        jax.ShapeDtypeStruct((n, 2 * D_MODEL), BF16),
    )
    vt_spec = pl.BlockSpec((1, tm // VT_CHUNK, KV_COLS, VT_CHUNK), lambda i: (i // nt, i % nt, 0, 0))
    out_specs = (
        pl.BlockSpec((tm, D_MODEL), tok),
        pl.BlockSpec((1, Q_COLS, tm), bt3),
        pl.BlockSpec((1, tm, KV_COLS), lambda i: (i // nt, i % nt, 0)),
        vt_spec,
        pl.BlockSpec((1, Q_COLS, tm), bt3),
        pl.BlockSpec((1, tm, KV_COLS), lambda i: (i // nt, i % nt, 0)),
        vt_spec,
        pl.BlockSpec((tm, 2 * D_MODEL), tok),
    )
    in_specs = [
        pl.BlockSpec((tm, D_MODEL), tok),
        _resident((1, D_MODEL)),
        _resident((D_MODEL, 2 * D_FF)),
        _resident((D_FF, D_MODEL)),
        _resident((1, D_MODEL)),
        _resident((FM_ROWS, D_MODEL)),
        _resident((D_MODEL, TM_COLS)),
        _resident((HEAD_DIM, 1)),
        _resident((HEAD_DIM, 1)),
        pl.BlockSpec((4 * ROT, tm), lambda i: (0, i % nt)),
    ]
    return pl.pallas_call(
        _front_kernel,
        grid=grid,
        in_specs=in_specs,
        out_specs=out_specs,
        out_shape=out_shape,
        compiler_params=pltpu.CompilerParams(
            dimension_semantics=("arbitrary",), vmem_limit_bytes=56 * 1024 * 1024),
        name="front",
    )(x, p["n1"], p["w1i"], p["w1o"], p["nm"], p["wfm"], p["wtm"], p["qn"], p["kn"], cs)


def _build_q_weights(q, wq_ref, tq):
    zeros = jnp.zeros((HEAD_DIM, tq), BF16)
    for hd in range(N_HEADS):
        qh = q[hd * HEAD_DIM:(hd + 1) * HEAD_DIM]
        blk = jnp.concatenate([qh, zeros], axis=0) if hd < GROUP else jnp.concatenate([zeros, qh], axis=0)
        wq_ref[:, hd * tq:(hd + 1) * tq] = blk


def _attn_a_kernel(q_ref, k_ref, v_ref, o_ref, wq_ref, s0_ref, s1_ref, t0_ref, t1_ref, m_ref, l_ref, acc_ref,
                   *, seq):
    tq = q_ref.shape[2]
    half = GROUP * tq
    tk = KEY_TILE_A
    cpt = tk // VT_CHUNK
    nk = seq // tk
    _build_q_weights(q_ref[0], wq_ref, tq)
    m_ref[...] = jnp.full(m_ref.shape, NEG_INF, F32)
    l_ref[...] = jnp.zeros(l_ref.shape, F32)
    acc_ref[...] = jnp.zeros(acc_ref.shape, F32)

    def scores(j, s_ref, t_ref):
        kt = k_ref[0, pl.ds(pl.multiple_of(j * tk, tk), tk), :]
        s = jnp.dot(kt, wq_ref[...], preferred_element_type=F32)
        s_ref[...] = s
        t_ref[...] = jnp.max(s, axis=0, keepdims=True)

    def softmax_pv(j, s_ref, t_ref):
        m_old = m_ref[...]
        m_new = jnp.maximum(m_old, t_ref[...])
        alpha = jnp.exp2(m_old - m_new)
        p = jnp.exp2(s_ref[...] - m_new)
        l_ref[...] = alpha * l_ref[...] + jnp.sum(p, axis=0, keepdims=True)
        m_ref[...] = m_new
        pb = p.astype(BF16)
        vt = jnp.concatenate([v_ref[0, j * cpt + c] for c in range(cpt)], axis=1)
        for kv in range(N_KV):
            cols = slice(kv * half, (kv + 1) * half)
            pv = jnp.dot(vt[kv * HEAD_DIM:(kv + 1) * HEAD_DIM], pb[:, cols], preferred_element_type=F32)
            acc_ref[kv] = acc_ref[kv] * alpha[:, cols] + pv

    scores(0, s0_ref, t0_ref)

    def body(jj, carry):
        j = 2 * jj
        scores(j + 1, s1_ref, t1_ref)
        softmax_pv(j, s0_ref, t0_ref)
        scores(jnp.minimum(j + 2, nk - 1), s0_ref, t0_ref)
        softmax_pv(j + 1, s1_ref, t1_ref)
        return carry

    lax.fori_loop(0, nk // 2, body, 0)
    inv_l = 1.0 / l_ref[...]
    rows = []
    for hd in range(N_HEADS):
        kv, g = divmod(hd, GROUP)
        cols = slice(g * tq, (g + 1) * tq)
        rows.append(acc_ref[kv][:, cols] * inv_l[:, hd * tq:(hd + 1) * tq])
    o_ref[0] = jnp.concatenate(rows, axis=0).T.astype(BF16)


def _attn_a(qt, k, vt, batch, seq):
    tq = Q_TILE_A
    assert (seq // KEY_TILE_A) % 2 == 0
    kernel = functools.partial(_attn_a_kernel, seq=seq)
    row = pltpu.VMEM((1, N_HEADS * tq), F32)
    tile = pltpu.VMEM((KEY_TILE_A, N_HEADS * tq), F32)
    return pl.pallas_call(
        kernel,
        grid=(batch, seq // tq),
        in_specs=[
            pl.BlockSpec((1, Q_COLS, tq), lambda b, i: (b, 0, i)),
            pl.BlockSpec((1, seq, KV_COLS), lambda b, i: (b, 0, 0)),
            pl.BlockSpec((1, seq // VT_CHUNK, KV_COLS, VT_CHUNK), lambda b, i: (b, 0, 0, 0)),
        ],
        out_specs=pl.BlockSpec((1, tq, Q_COLS), lambda b, i: (b, i, 0)),
        out_shape=jax.ShapeDtypeStruct((batch, seq, Q_COLS), BF16),
        scratch_shapes=[
            pltpu.VMEM((KV_COLS, N_HEADS * tq), BF16),
            tile, tile,
            row, row,
            row, row,
            pltpu.VMEM((N_KV, HEAD_DIM, GROUP * tq), F32),
        ],
        compiler_params=pltpu.CompilerParams(
            dimension_semantics=("arbitrary", "arbitrary"), vmem_limit_bytes=40 * 1024 * 1024),
        name="attn_a",
    )(qt, k, vt)


def _attn_b_kernel(rb_ref, q_ref, k_ref, v_ref, bucket_ref, sink_ref, o_ref, wq_ref, bias_ref, *, seq):
    tq = Q_BLOCK
    nb = seq // tq
    half = GROUP * tq
    i = pl.program_id(1)

    @pl.when((pl.program_id(0) == 0) & (i == 0))
    def _():
        for jb in range(3):
            bucket = bucket_ref[jb]
            key = lax.broadcasted_iota(jnp.int32, (tq, tq), 0)
            qry = lax.broadcasted_iota(jnp.int32, (tq, tq), 1)
            rel = (jb - 1) * tq + key - qry
            inside = jnp.abs(rel) <= WINDOW
            for hd in range(N_HEADS):
                tbl = jnp.zeros((tq, tq), F32)
                for b in range(N_BUCKETS):
                    tbl = jnp.where(bucket == b, rb_ref[b, hd], tbl)
                bias_ref[jb, :, hd * tq:(hd + 1) * tq] = jnp.where(inside, tbl, NEG_INF)

    _build_q_weights(q_ref[0], wq_ref, tq)
    parts = []
    idxs = []
    for jb in range(3):
        blk = i - 1 + jb
        valid = (blk >= 0) & (blk < nb)
        idx = jnp.clip(blk, 0, nb - 1)
        kt = k_ref[0, pl.ds(pl.multiple_of(idx * tq, tq), tq), :]
        s = jnp.dot(kt, wq_ref[...], preferred_element_type=F32) + bias_ref[jb]
        parts.append(jnp.where(valid, s, NEG_INF))
        idxs.append(idx)
    s = jnp.concatenate(parts, axis=0)
    sink = sink_ref[...]
    m = jnp.maximum(jnp.max(s, axis=0, keepdims=True), sink)
    p = jnp.exp(s - m)
    inv_l = 1.0 / (jnp.sum(p, axis=0, keepdims=True) + jnp.exp(sink - m))
    pb = p.astype(BF16)
    rows = []
    for kv in range(N_KV):
        acc = jnp.zeros((HEAD_DIM, half), F32)
        for jb in range(3):
            vt = v_ref[0, idxs[jb]][kv * HEAD_DIM:(kv + 1) * HEAD_DIM]
            acc = acc + jnp.dot(vt, pb[jb * tq:(jb + 1) * tq, kv * half:(kv + 1) * half],
                                preferred_element_type=F32)
        acc = acc * inv_l[:, kv * half:(kv + 1) * half]
        for g in range(GROUP):
            rows.append(acc[:, g * tq:(g + 1) * tq])
    o_ref[0] = jnp.concatenate(rows, axis=0).T.astype(BF16)


def _attn_b(qt, k, vt, rel_bias, bucket_t, sink_row, batch, seq):
    tq = Q_BLOCK
    kernel = functools.partial(_attn_b_kernel, seq=seq)
    return pl.pallas_call(
        kernel,
        grid=(batch, seq // tq),
        in_specs=[
            pl.BlockSpec(memory_space=pltpu.SMEM),
            pl.BlockSpec((1, Q_COLS, tq), lambda b, i: (b, 0, i)),
            pl.BlockSpec((1, seq, KV_COLS), lambda b, i: (b, 0, 0)),
            pl.BlockSpec((1, seq // VT_CHUNK, KV_COLS, VT_CHUNK), lambda b, i: (b, 0, 0, 0)),
            pl.BlockSpec((3, tq, tq), lambda b, i: (0, 0, 0)),
            pl.BlockSpec((1, N_HEADS * tq), lambda b, i: (0, 0)),
        ],
        out_specs=pl.BlockSpec((1, tq, Q_COLS), lambda b, i: (b, i, 0)),
        out_shape=jax.ShapeDtypeStruct((batch, seq, Q_COLS), BF16),
        scratch_shapes=[
            pltpu.VMEM((KV_COLS, N_HEADS * tq), BF16),
            pltpu.VMEM((3, tq, N_HEADS * tq), F32),
        ],
        compiler_params=pltpu.CompilerParams(
            dimension_semantics=("arbitrary", "arbitrary"), vmem_limit_bytes=40 * 1024 * 1024),
        name="attn_b",
    )(rel_bias, qt, k, vt, bucket_t, sink_row)


def _back_kernel(x1_ref, oa_ref, ob_ref, g_ref, wba_ref, wbb_ref, wo_ref, n2_ref, w2i_ref, w2o_ref, nf_ref, y_ref):
    ya = jnp.dot(oa_ref[...], wba_ref[...], preferred_element_type=F32)
    yb = jnp.dot(ob_ref[...], wbb_ref[...], preferred_element_type=F32)
    g = g_ref[...].astype(F32)
    merged = _sigmoid(g[:, :D_MODEL]) * ya + _sigmoid(g[:, D_MODEL:]) * yb
    x2 = x1_ref[...] + jnp.dot(merged.astype(BF16), wo_ref[...], preferred_element_type=F32)
    x3 = _swiglu_half_step(x2, n2_ref, w2i_ref, w2o_ref)
    y_ref[...] = _rms(x3, nf_ref[...])


def _back(x1, oa, ob, g, p):
    tm = TOKEN_TILE
    n = x1.shape[0]
    tok = lambda i: (i, 0)
    return pl.pallas_call(
        _back_kernel,
        grid=(n // tm,),
        in_specs=[
            pl.BlockSpec((tm, D_MODEL), tok),
            pl.BlockSpec((tm, Q_COLS), tok),
            pl.BlockSpec((tm, Q_COLS), tok),
            pl.BlockSpec((tm, 2 * D_MODEL), tok),
            _resident((Q_COLS, D_MODEL)),
            _resident((Q_COLS, D_MODEL)),
            _resident((D_MODEL, D_MODEL)),
            _resident((1, D_MODEL)),
            _resident((D_MODEL, 2 * D_FF)),
            _resident((D_FF, D_MODEL)),
            _resident((1, D_MODEL)),
        ],
        out_specs=pl.BlockSpec((tm, D_MODEL), tok),
        out_shape=jax.ShapeDtypeStruct((n, D_MODEL), F32),
        compiler_params=pltpu.CompilerParams(
            dimension_semantics=("arbitrary",), vmem_limit_bytes=56 * 1024 * 1024),
        name="back",
    )(x1, oa, ob, g, p["wba"], p["wbb"], p["wo"], p["n2"], p["w2i"], p["w2o"], p["nf"])


def _t5_bucket(rel):
    nb = N_BUCKETS // 2
    max_exact = nb // 2
    ret = jnp.where(rel > 0, nb, 0)
    n = jnp.abs(rel)
    large = max_exact + (jnp.log(jnp.maximum(n, 1).astype(F32) / max_exact)
                         / math.log(MAX_DISTANCE / max_exact) * (nb - max_exact)).astype(jnp.int32)
    large = jnp.minimum(large, nb - 1)
    return ret + jnp.where(n < max_exact, n, large)


def _rope_table(seq):
    rows = seq // GRID_W
    grid_r, grid_c = jnp.meshgrid(jnp.arange(rows, dtype=F32), jnp.arange(GRID_W, dtype=F32), indexing="ij")
    pos = jnp.stack([grid_r.reshape(-1), grid_c.reshape(-1)], axis=-1)
    inv = ROPE_THETA ** (-jnp.arange(0, AXIS_DIM, 2, dtype=F32) / AXIS_DIM)
    ang = pos[:, :, None] * inv
    cos = jnp.cos(ang)
    sin = jnp.sin(ang)
    return jnp.concatenate([cos[:, 0].T, sin[:, 0].T, cos[:, 1].T, sin[:, 1].T], axis=0)


def _trunk(x, p, bucket_t, sink_row, rel_bias):
    batch, seq, _ = x.shape
    cs = _rope_table(seq)
    x1, qa, ka, va, qb, kb, vb, g = _front(x.reshape(batch * seq, D_MODEL), p, cs, batch, seq)
    oa = _attn_a(qa, ka, va, batch, seq)
    ob = _attn_b(qb, kb, vb, rel_bias, bucket_t, sink_row, batch, seq)
    y = _back(x1, oa.reshape(batch * seq, Q_COLS), ob.reshape(batch * seq, Q_COLS), g, p)
    return y.reshape(batch, seq, D_MODEL)


def kernel(x_prompt, x_sample, norm_ffn1, w_ffn1_in, w_ffn1_out, norm_mix, w_in, q_norm_a, k_norm_a, sink_b,
           w_branch_a, w_branch_b, w_out, norm_ffn2, w_ffn2_in, w_ffn2_out, rel_bias, norm_final):
    assert norm_ffn1.shape[0] == 1, "single-layer trunk"
    wi = w_in[0]
    c = 0
    parts = {}
    for name, width in (("qa", Q_COLS), ("ka", KV_COLS), ("va", KV_COLS), ("qb", Q_COLS), ("kb", KV_COLS),
                        ("vb", KV_COLS), ("ga", D_MODEL), ("gb", D_MODEL)):
        parts[name] = wi[:, c:c + width]
        c += width
    wfm = jnp.concatenate([parts[k] for k in ("qa", "ka", "va", "qb", "vb")], axis=1).T.astype(BF16)
    wtm = jnp.concatenate([parts[k] for k in ("kb", "ga", "gb")], axis=1).astype(BF16)
    p = dict(
        n1=norm_ffn1[0].reshape(1, D_MODEL), w1i=w_ffn1_in[0].astype(BF16), w1o=w_ffn1_out[0].astype(BF16),
        nm=norm_mix[0].reshape(1, D_MODEL), wfm=wfm, wtm=wtm,
        qn=q_norm_a[0].reshape(HEAD_DIM, 1), kn=k_norm_a[0].reshape(HEAD_DIM, 1),
        wba=w_branch_a[0].astype(BF16), wbb=w_branch_b[0].astype(BF16), wo=w_out[0].astype(BF16),
        n2=norm_ffn2[0].reshape(1, D_MODEL), w2i=w_ffn2_in[0].astype(BF16), w2o=w_ffn2_out[0].astype(BF16),
        nf=norm_final.reshape(1, D_MODEL),
    )
    key = jnp.arange(3 * Q_BLOCK)[:, None] - Q_BLOCK
    qry = jnp.arange(Q_BLOCK)[None, :]
    bucket_t = _t5_bucket(key - qry).astype(jnp.int32).reshape(3, Q_BLOCK, Q_BLOCK)
    sink_row = jnp.repeat(sink_b[0].astype(F32), Q_BLOCK).reshape(1, N_HEADS * Q_BLOCK)
    rb = rel_bias.astype(F32)
    y_prompt = _trunk(x_prompt, p, bucket_t, sink_row, rb)
    y_sample = _trunk(x_sample, p, bucket_t, sink_row, rb)
    return (y_prompt, y_sample)
```

```python
import functools
import math

import jax
import jax.numpy as jnp
from jax import lax
from jax.experimental import pallas as pl
from jax.experimental.pallas import tpu as pltpu

D_MODEL = 1024
GRID_W = 64
HEAD_DIM = 64
N_HEADS = 8
N_KV = 2
GROUP = N_HEADS // N_KV
Q_BLOCK = 128
WINDOW = 128
N_BUCKETS = 32
MAX_DISTANCE = 128
ROPE_THETA = 10000.0
AXIS_DIM = HEAD_DIM // 2
ROT = AXIS_DIM // 2
D_FF = 2816
EPS = 1e-6
NEG_INF = -1e30
Q_COLS = N_HEADS * HEAD_DIM
KV_COLS = N_KV * HEAD_DIM

F32 = jnp.float32
BF16 = jnp.bfloat16

V7X_LANES = 128
V7X_VMEM_BYTES = 64 * 1024 * 1024

TOKEN_TILE = 256
KEY_TILE_A = 256
Q_TILE_A = 128
VT_CHUNK = 128
SUM_ROWS = 16
DIRECT_TILES_PER_TRIP = 4
MAX_DIRECT_LOGIT = 32.0

FM_ROWS = Q_COLS + KV_COLS + KV_COLS + Q_COLS + KV_COLS
TM_COLS = KV_COLS + 2 * D_MODEL


def _resident(shape):
    nd = len(shape)
    return pl.BlockSpec(shape, lambda *_: (0,) * nd, pipeline_mode=pl.Buffered(1))


def _rms(x, g):
    ms = jnp.mean(x * x, axis=-1, keepdims=True)
    return x * lax.rsqrt(ms + EPS) * g


def _sigmoid(x):
    return 1.0 / (1.0 + jnp.exp(-x))


def _swiglu_half_step(x, g_ref, w_in_ref, w_out_ref):
    h = _rms(x, g_ref[...]).astype(BF16)
    ab = jnp.dot(h, w_in_ref[...], preferred_element_type=F32)
    a = ab[:, :D_FF]
    b = ab[:, D_FF:]
    act = (a * _sigmoid(a) * b).astype(BF16)
    return x + 0.5 * jnp.dot(act, w_out_ref[...], preferred_element_type=F32)


def _head_norm_rope_t(z, gain, cs):
    ms = jnp.mean(z * z, axis=0, keepdims=True)
    z = z * lax.rsqrt(ms + EPS) * gain
    cr, sr, cc, sc = cs[0:ROT], cs[ROT:2 * ROT], cs[2 * ROT:3 * ROT], cs[3 * ROT:4 * ROT]
    x1r, x2r = z[0:ROT], z[ROT:2 * ROT]
    x1c, x2c = z[2 * ROT:3 * ROT], z[3 * ROT:4 * ROT]
    return jnp.concatenate(
        [x1r * cr - x2r * sr, x2r * cr + x1r * sr, x1c * cc - x2c * sc, x2c * cc + x1c * sc], axis=0)


def _front_kernel(x_ref, n1_ref, w1i_ref, w1o_ref, nm_ref, wfm_ref, wtm_ref, qn_ref, kn_ref, cs_ref,
                  x1_ref, qa_ref, ka_ref, va_ref, qb_ref, kb_ref, vb_ref, g_ref):
    tm = x_ref.shape[0]
    x1 = _swiglu_half_step(x_ref[...], n1_ref, w1i_ref, w1o_ref)
    x1_ref[...] = x1
    h = _rms(x1, nm_ref[...]).astype(BF16)
    pt = lax.dot_general(wfm_ref[...], h, (((1,), (1,)), ((), ())), preferred_element_type=F32)
    ptm = jnp.dot(h, wtm_ref[...], preferred_element_type=F32)
    cs = cs_ref[...]
    qn = qn_ref[...]
    kn = kn_ref[...]
    scale = 1.0 / math.sqrt(HEAD_DIM)
    scale_a = scale * math.log2(math.e)
    for hd in range(N_HEADS):
        z = pt[hd * HEAD_DIM:(hd + 1) * HEAD_DIM]
        qa_ref[0, hd * HEAD_DIM:(hd + 1) * HEAD_DIM, :] = (_head_norm_rope_t(z, qn, cs) * scale_a).astype(BF16)
    ka_t = jnp.concatenate(
        [_head_norm_rope_t(pt[Q_COLS + kv * HEAD_DIM:Q_COLS + (kv + 1) * HEAD_DIM], kn, cs) for kv in range(N_KV)],
        axis=0)
    ka_ref[0] = ka_t.T.astype(BF16)
    off = Q_COLS + KV_COLS
    va = pt[off:off + KV_COLS].astype(BF16)
    off += KV_COLS
    qb_ref[0] = (pt[off:off + Q_COLS] * scale).astype(BF16)
    off += Q_COLS
    vb = pt[off:off + KV_COLS].astype(BF16)
    for c in range(tm // VT_CHUNK):
        va_ref[0, c] = va[:, c * VT_CHUNK:(c + 1) * VT_CHUNK]
        vb_ref[0, c] = vb[:, c * VT_CHUNK:(c + 1) * VT_CHUNK]
    kb_ref[0] = ptm[:, :KV_COLS].astype(BF16)
    g_ref[...] = ptm[:, KV_COLS:].astype(BF16)


def _front(x, p, cs, batch, seq):
    tm = TOKEN_TILE
    n = batch * seq
    nt = seq // tm
    grid = (n // tm,)
    tok = lambda i: (i, 0)
    bt3 = lambda i: (i // nt, 0, i % nt)
    out_shape = (
        jax.ShapeDtypeStruct((n, D_MODEL), F32),
        jax.ShapeDtypeStruct((batch, Q_COLS, seq), BF16),
        jax.ShapeDtypeStruct((batch, seq, KV_COLS), BF16),
        jax.ShapeDtypeStruct((batch, seq // VT_CHUNK, KV_COLS, VT_CHUNK), BF16),
        jax.ShapeDtypeStruct((batch, Q_COLS, seq), BF16),
        jax.ShapeDtypeStruct((batch, seq, KV_COLS), BF16),
        jax.ShapeDtypeStruct((batch, seq // VT_CHUNK, KV_COLS, VT_CHUNK), BF16),
        jax.ShapeDtypeStruct((n, 2 * D_MODEL), BF16),
    )
    vt_spec = pl.BlockSpec((1, tm // VT_CHUNK, KV_COLS, VT_CHUNK), lambda i: (i // nt, i % nt, 0, 0))
    out_specs = (
        pl.BlockSpec((tm, D_MODEL), tok),
        pl.BlockSpec((1, Q_COLS, tm), bt3),
        pl.BlockSpec((1, tm, KV_COLS), lambda i: (i // nt, i % nt, 0)),
        vt_spec,
        pl.BlockSpec((1, Q_COLS, tm), bt3),
        pl.BlockSpec((1, tm, KV_COLS), lambda i: (i // nt, i % nt, 0)),
        vt_spec,
        pl.BlockSpec((tm, 2 * D_MODEL), tok),
    )
    in_specs = [
        pl.BlockSpec((tm, D_MODEL), tok),
        _resident((1, D_MODEL)),
        _resident((D_MODEL, 2 * D_FF)),
        _resident((D_FF, D_MODEL)),
        _resident((1, D_MODEL)),
        _resident((FM_ROWS, D_MODEL)),
        _resident((D_MODEL, TM_COLS)),
        _resident((HEAD_DIM, 1)),
        _resident((HEAD_DIM, 1)),
        pl.BlockSpec((4 * ROT, tm), lambda i: (0, i % nt)),
    ]
    return pl.pallas_call(
        _front_kernel,
        grid=grid,
        in_specs=in_specs,
        out_specs=out_specs,
        out_shape=out_shape,
        compiler_params=pltpu.CompilerParams(
            dimension_semantics=("arbitrary",), vmem_limit_bytes=56 * 1024 * 1024),
        name="front",
    )(x, p["n1"], p["w1i"], p["w1o"], p["nm"], p["wfm"], p["wtm"], p["qn"], p["kn"], cs)


def _build_q_weights(q, wq_ref, tq):
    zeros = jnp.zeros((HEAD_DIM, tq), BF16)
    for hd in range(N_HEADS):
        qh = q[hd * HEAD_DIM:(hd + 1) * HEAD_DIM]
        blk = jnp.concatenate([qh, zeros], axis=0) if hd < GROUP else jnp.concatenate([zeros, qh], axis=0)
        wq_ref[:, hd * tq:(hd + 1) * tq] = blk


def _attn_a_online_kernel(q_ref, k_ref, v_ref, o_ref, wq_ref, s0_ref, s1_ref, t0_ref, t1_ref, m_ref, l_ref,
                          acc_ref, *, seq):
    tq = q_ref.shape[2]
    half = GROUP * tq
    tk = KEY_TILE_A
    cpt = tk // VT_CHUNK
    nk = seq // tk
    _build_q_weights(q_ref[0], wq_ref, tq)
    m_ref[...] = jnp.full(m_ref.shape, NEG_INF, F32)
    l_ref[...] = jnp.zeros(l_ref.shape, F32)
    acc_ref[...] = jnp.zeros(acc_ref.shape, F32)

    def scores(j, s_ref, t_ref):
        kt = k_ref[0, pl.ds(pl.multiple_of(j * tk, tk), tk), :]
        s = jnp.dot(kt, wq_ref[...], preferred_element_type=F32)
        s_ref[...] = s
        t_ref[...] = jnp.max(s, axis=0, keepdims=True)

    def softmax_pv(j, s_ref, t_ref):
        m_old = m_ref[...]
        m_new = jnp.maximum(m_old, t_ref[...])
        alpha = jnp.exp2(m_old - m_new)
        p = jnp.exp2(s_ref[...] - m_new)
        l_ref[...] = alpha * l_ref[...] + jnp.sum(p, axis=0, keepdims=True)
        m_ref[...] = m_new
        pb = p.astype(BF16)
        vt = jnp.concatenate([v_ref[0, j * cpt + c] for c in range(cpt)], axis=1)
        for kv in range(N_KV):
            cols = slice(kv * half, (kv + 1) * half)
            pv = jnp.dot(vt[kv * HEAD_DIM:(kv + 1) * HEAD_DIM], pb[:, cols], preferred_element_type=F32)
            acc_ref[kv] = acc_ref[kv] * alpha[:, cols] + pv

    scores(0, s0_ref, t0_ref)

    def body(jj, carry):
        j = 2 * jj
        scores(j + 1, s1_ref, t1_ref)
        softmax_pv(j, s0_ref, t0_ref)
        scores(jnp.minimum(j + 2, nk - 1), s0_ref, t0_ref)
        softmax_pv(j + 1, s1_ref, t1_ref)
        return carry

    lax.fori_loop(0, nk // 2, body, 0)
    inv_l = 1.0 / l_ref[...]
    rows = []
    for hd in range(N_HEADS):
        kv, g = divmod(hd, GROUP)
        cols = slice(g * tq, (g + 1) * tq)
        rows.append(acc_ref[kv][:, cols] * inv_l[:, hd * tq:(hd + 1) * tq])
    o_ref[0] = jnp.concatenate(rows, axis=0).T.astype(BF16)


def _attn_a_online(qt, k, vt, batch, seq):
    tq = Q_TILE_A
    assert (seq // KEY_TILE_A) % 2 == 0
    kernel = functools.partial(_attn_a_online_kernel, seq=seq)
    row = pltpu.VMEM((1, N_HEADS * tq), F32)
    tile = pltpu.VMEM((KEY_TILE_A, N_HEADS * tq), F32)
    return pl.pallas_call(
        kernel,
        grid=(batch, seq // tq),
        in_specs=[
            pl.BlockSpec((1, Q_COLS, tq), lambda b, i: (b, 0, i)),
            pl.BlockSpec((1, seq, KV_COLS), lambda b, i: (b, 0, 0)),
            pl.BlockSpec((1, seq // VT_CHUNK, KV_COLS, VT_CHUNK), lambda b, i: (b, 0, 0, 0)),
        ],
        out_specs=pl.BlockSpec((1, tq, Q_COLS), lambda b, i: (b, i, 0)),
        out_shape=jax.ShapeDtypeStruct((batch, seq, Q_COLS), BF16),
        scratch_shapes=[
            pltpu.VMEM((KV_COLS, N_HEADS * tq), BF16),
            tile, tile,
            row, row,
            row, row,
            pltpu.VMEM((N_KV, HEAD_DIM, GROUP * tq), F32),
        ],
        compiler_params=pltpu.CompilerParams(
            dimension_semantics=("arbitrary", "arbitrary"), vmem_limit_bytes=40 * 1024 * 1024),
        name="attn_a_online",
    )(qt, k, vt)


def _attn_a_direct_kernel(q_ref, k_ref, v_ref, o_ref, wq_ref, p0_ref, p1_ref, acc_ref, *, seq):
    tq = q_ref.shape[2]
    half = GROUP * tq
    tk = KEY_TILE_A
    cpt = tk // VT_CHUNK
    nk = seq // tk
    per_trip = DIRECT_TILES_PER_TRIP
    p_refs = (p0_ref, p1_ref)
    _build_q_weights(q_ref[0], wq_ref, tq)
    acc_ref[...] = jnp.zeros(acc_ref.shape, F32)

    def probs(j, p_ref):
        kt = k_ref[0, pl.ds(pl.multiple_of(j * tk, tk), tk), :]
        s = jnp.dot(kt, wq_ref[...], preferred_element_type=F32)
        p_ref[...] = jnp.exp2(s).astype(BF16)

    def pv(j, p_ref):
        vt = jnp.concatenate([v_ref[0, j * cpt + c] for c in range(cpt)], axis=1)
        ones = jnp.ones((SUM_ROWS, tk), BF16)
        for kv in range(N_KV):
            cols = slice(kv * half, (kv + 1) * half)
            lhs = jnp.concatenate([vt[kv * HEAD_DIM:(kv + 1) * HEAD_DIM], ones], axis=0)
            acc_ref[kv] += jnp.dot(lhs, p_ref[:, cols], preferred_element_type=F32)

    def trip(j0, last):
        for u in range(per_trip):
            if not (last and u == per_trip - 1):
                probs(j0 + u + 1, p_refs[(u + 1) % 2])
            pv(j0 + u, p_refs[u % 2])

    probs(0, p0_ref)

    def body(jj, carry):
        trip(per_trip * jj, False)
        return carry

    lax.fori_loop(0, nk // per_trip - 1, body, 0)
    trip(nk - per_trip, True)
    rows = []
    for hd in range(N_HEADS):
        kv, g = divmod(hd, GROUP)
        cols = slice(g * tq, (g + 1) * tq)
        acc = acc_ref[kv]
        rows.append(acc[:HEAD_DIM, cols] * (1.0 / acc[HEAD_DIM:HEAD_DIM + 1, cols]))
    o_ref[0] = jnp.concatenate(rows, axis=0).T.astype(BF16)


def _attn_a_direct(qt, k, vt, batch, seq):
    tq = Q_TILE_A
    assert DIRECT_TILES_PER_TRIP % 2 == 0 and (seq // KEY_TILE_A) % DIRECT_TILES_PER_TRIP == 0
    kernel = functools.partial(_attn_a_direct_kernel, seq=seq)
    p_tile = pltpu.VMEM((KEY_TILE_A, N_HEADS * tq), BF16)
    return pl.pallas_call(
        kernel,
        grid=(batch, seq // tq),
        in_specs=[
            pl.BlockSpec((1, Q_COLS, tq), lambda b, i: (b, 0, i)),
            pl.BlockSpec((1, seq, KV_COLS), lambda b, i: (b, 0, 0)),
            pl.BlockSpec((1, seq // VT_CHUNK, KV_COLS, VT_CHUNK), lambda b, i: (b, 0, 0, 0)),
        ],
        out_specs=pl.BlockSpec((1, tq, Q_COLS), lambda b, i: (b, i, 0)),
        out_shape=jax.ShapeDtypeStruct((batch, seq, Q_COLS), BF16),
        scratch_shapes=[
            pltpu.VMEM((KV_COLS, N_HEADS * tq), BF16),
            p_tile, p_tile,
            pltpu.VMEM((N_KV, HEAD_DIM + SUM_ROWS, GROUP * tq), F32),
        ],
        compiler_params=pltpu.CompilerParams(
            dimension_semantics=("arbitrary", "arbitrary"), vmem_limit_bytes=40 * 1024 * 1024),
        name="attn_a_direct",
    )(qt, k, vt)


def _attn_a(qt, k, vt, direct_ok, batch, seq):
    return lax.cond(direct_ok,
                    lambda: _attn_a_direct(qt, k, vt, batch, seq),
                    lambda: _attn_a_online(qt, k, vt, batch, seq))


def _attn_b_kernel(rb_ref, q_ref, k_ref, v_ref, bucket_ref, sink_ref, o_ref, wq_ref, bias_ref, *, seq):
    tq = Q_BLOCK
    nb = seq // tq
    half = GROUP * tq
    i = pl.program_id(1)

    @pl.when((pl.program_id(0) == 0) & (i == 0))
    def _():
        for jb in range(3):
            bucket = bucket_ref[jb]
            key = lax.broadcasted_iota(jnp.int32, (tq, tq), 0)
            qry = lax.broadcasted_iota(jnp.int32, (tq, tq), 1)
            rel = (jb - 1) * tq + key - qry
            inside = jnp.abs(rel) <= WINDOW
            for hd in range(N_HEADS):
                tbl = jnp.zeros((tq, tq), F32)
                for b in range(N_BUCKETS):
                    tbl = jnp.where(bucket == b, rb_ref[b, hd], tbl)
                bias_ref[jb, :, hd * tq:(hd + 1) * tq] = jnp.where(inside, tbl, NEG_INF)

    _build_q_weights(q_ref[0], wq_ref, tq)
    parts = []
    idxs = []
    for jb in range(3):
        blk = i - 1 + jb
        valid = (blk >= 0) & (blk < nb)
        idx = jnp.clip(blk, 0, nb - 1)
        kt = k_ref[0, pl.ds(pl.multiple_of(idx * tq, tq), tq), :]
        s = jnp.dot(kt, wq_ref[...], preferred_element_type=F32) + bias_ref[jb]
        parts.append(jnp.where(valid, s, NEG_INF))
        idxs.append(idx)
    s = jnp.concatenate(parts, axis=0)
    sink = sink_ref[...]
    m = jnp.maximum(jnp.max(s, axis=0, keepdims=True), sink)
    p = jnp.exp(s - m)
    inv_l = 1.0 / (jnp.sum(p, axis=0, keepdims=True) + jnp.exp(sink - m))
    pb = p.astype(BF16)
    rows = []
    for kv in range(N_KV):
        acc = jnp.zeros((HEAD_DIM, half), F32)
        for jb in range(3):
            vt = v_ref[0, idxs[jb]][kv * HEAD_DIM:(kv + 1) * HEAD_DIM]
            acc = acc + jnp.dot(vt, pb[jb * tq:(jb + 1) * tq, kv * half:(kv + 1) * half],
                                preferred_element_type=F32)
        acc = acc * inv_l[:, kv * half:(kv + 1) * half]
        for g in range(GROUP):
            rows.append(acc[:, g * tq:(g + 1) * tq])
    o_ref[0] = jnp.concatenate(rows, axis=0).T.astype(BF16)


def _attn_b(qt, k, vt, rel_bias, bucket_t, sink_row, batch, seq):
    tq = Q_BLOCK
    kernel = functools.partial(_attn_b_kernel, seq=seq)
    return pl.pallas_call(
        kernel,
        grid=(batch, seq // tq),
        in_specs=[
            pl.BlockSpec(memory_space=pltpu.SMEM),
            pl.BlockSpec((1, Q_COLS, tq), lambda b, i: (b, 0, i)),
            pl.BlockSpec((1, seq, KV_COLS), lambda b, i: (b, 0, 0)),
            pl.BlockSpec((1, seq // VT_CHUNK, KV_COLS, VT_CHUNK), lambda b, i: (b, 0, 0, 0)),
            pl.BlockSpec((3, tq, tq), lambda b, i: (0, 0, 0)),
            pl.BlockSpec((1, N_HEADS * tq), lambda b, i: (0, 0)),
        ],
        out_specs=pl.BlockSpec((1, tq, Q_COLS), lambda b, i: (b, i, 0)),
        out_shape=jax.ShapeDtypeStruct((batch, seq, Q_COLS), BF16),
        scratch_shapes=[
            pltpu.VMEM((KV_COLS, N_HEADS * tq), BF16),
            pltpu.VMEM((3, tq, N_HEADS * tq), F32),
        ],
        compiler_params=pltpu.CompilerParams(
            dimension_semantics=("arbitrary", "arbitrary"), vmem_limit_bytes=40 * 1024 * 1024),
        name="attn_b",
    )(rel_bias, qt, k, vt, bucket_t, sink_row)


def _back_kernel(x1_ref, oa_ref, ob_ref, g_ref, wba_ref, wbb_ref, wo_ref, n2_ref, w2i_ref, w2o_ref, nf_ref, y_ref):
    ya = jnp.dot(oa_ref[...], wba_ref[...], preferred_element_type=F32)
    yb = jnp.dot(ob_ref[...], wbb_ref[...], preferred_element_type=F32)
    g = g_ref[...].astype(F32)
    merged = _sigmoid(g[:, :D_MODEL]) * ya + _sigmoid(g[:, D_MODEL:]) * yb
    x2 = x1_ref[...] + jnp.dot(merged.astype(BF16), wo_ref[...], preferred_element_type=F32)
    x3 = _swiglu_half_step(x2, n2_ref, w2i_ref, w2o_ref)
    y_ref[...] = _rms(x3, nf_ref[...])


def _back(x1, oa, ob, g, p):
    tm = TOKEN_TILE
    n = x1.shape[0]
    tok = lambda i: (i, 0)
    return pl.pallas_call(
        _back_kernel,
        grid=(n // tm,),
        in_specs=[
            pl.BlockSpec((tm, D_MODEL), tok),
            pl.BlockSpec((tm, Q_COLS), tok),
            pl.BlockSpec((tm, Q_COLS), tok),
            pl.BlockSpec((tm, 2 * D_MODEL), tok),
            _resident((Q_COLS, D_MODEL)),
            _resident((Q_COLS, D_MODEL)),
            _resident((D_MODEL, D_MODEL)),
            _resident((1, D_MODEL)),
            _resident((D_MODEL, 2 * D_FF)),
            _resident((D_FF, D_MODEL)),
            _resident((1, D_MODEL)),
        ],
        out_specs=pl.BlockSpec((tm, D_MODEL), tok),
        out_shape=jax.ShapeDtypeStruct((n, D_MODEL), F32),
        compiler_params=pltpu.CompilerParams(
            dimension_semantics=("arbitrary",), vmem_limit_bytes=56 * 1024 * 1024),
        name="back",
    )(x1, oa, ob, g, p["wba"], p["wbb"], p["wo"], p["n2"], p["w2i"], p["w2o"], p["nf"])


def _t5_bucket(rel):
    nb = N_BUCKETS // 2
    max_exact = nb // 2
    ret = jnp.where(rel > 0, nb, 0)
    n = jnp.abs(rel)
    large = max_exact + (jnp.log(jnp.maximum(n, 1).astype(F32) / max_exact)
                         / math.log(MAX_DISTANCE / max_exact) * (nb - max_exact)).astype(jnp.int32)
    large = jnp.minimum(large, nb - 1)
    return ret + jnp.where(n < max_exact, n, large)


def _rope_table(seq):
    rows = seq // GRID_W
    grid_r, grid_c = jnp.meshgrid(jnp.arange(rows, dtype=F32), jnp.arange(GRID_W, dtype=F32), indexing="ij")
    pos = jnp.stack([grid_r.reshape(-1), grid_c.reshape(-1)], axis=-1)
    inv = ROPE_THETA ** (-jnp.arange(0, AXIS_DIM, 2, dtype=F32) / AXIS_DIM)
    ang = pos[:, :, None] * inv
    cos = jnp.cos(ang)
    sin = jnp.sin(ang)
    return jnp.concatenate([cos[:, 0].T, sin[:, 0].T, cos[:, 1].T, sin[:, 1].T], axis=0)


def _trunk(x, p, bucket_t, sink_row, rel_bias, direct_ok):
    batch, seq, _ = x.shape
    cs = _rope_table(seq)
    x1, qa, ka, va, qb, kb, vb, g = _front(x.reshape(batch * seq, D_MODEL), p, cs, batch, seq)
    oa = _attn_a(qa, ka, va, direct_ok, batch, seq)
    ob = _attn_b(qb, kb, vb, rel_bias, bucket_t, sink_row, batch, seq)
    y = _back(x1, oa.reshape(batch * seq, Q_COLS), ob.reshape(batch * seq, Q_COLS), g, p)
    return y.reshape(batch, seq, D_MODEL)


def kernel(x_prompt, x_sample, norm_ffn1, w_ffn1_in, w_ffn1_out, norm_mix, w_in, q_norm_a, k_norm_a, sink_b,
           w_branch_a, w_branch_b, w_out, norm_ffn2, w_ffn2_in, w_ffn2_out, rel_bias, norm_final):
    assert norm_ffn1.shape[0] == 1, "single-layer trunk"
    wi = w_in[0]
    c = 0
    parts = {}
    for name, width in (("qa", Q_COLS), ("ka", KV_COLS), ("va", KV_COLS), ("qb", Q_COLS), ("kb", KV_COLS),
                        ("vb", KV_COLS), ("ga", D_MODEL), ("gb", D_MODEL)):
        parts[name] = wi[:, c:c + width]
        c += width
    wfm = jnp.concatenate([parts[k] for k in ("qa", "ka", "va", "qb", "vb")], axis=1).T.astype(BF16)
    wtm = jnp.concatenate([parts[k] for k in ("kb", "ga", "gb")], axis=1).astype(BF16)
    p = dict(
        n1=norm_ffn1[0].reshape(1, D_MODEL), w1i=w_ffn1_in[0].astype(BF16), w1o=w_ffn1_out[0].astype(BF16),
        nm=norm_mix[0].reshape(1, D_MODEL), wfm=wfm, wtm=wtm,
        qn=q_norm_a[0].reshape(HEAD_DIM, 1), kn=k_norm_a[0].reshape(HEAD_DIM, 1),
        wba=w_branch_a[0].astype(BF16), wbb=w_branch_b[0].astype(BF16), wo=w_out[0].astype(BF16),
        n2=norm_ffn2[0].reshape(1, D_MODEL), w2i=w_ffn2_in[0].astype(BF16), w2o=w_ffn2_out[0].astype(BF16),
        nf=norm_final.reshape(1, D_MODEL),
    )
    key = jnp.arange(3 * Q_BLOCK)[:, None] - Q_BLOCK
    qry = jnp.arange(Q_BLOCK)[None, :]
    bucket_t = _t5_bucket(key - qry).astype(jnp.int32).reshape(3, Q_BLOCK, Q_BLOCK)
    sink_row = jnp.repeat(sink_b[0].astype(F32), Q_BLOCK).reshape(1, N_HEADS * Q_BLOCK)
    rb = rel_bias.astype(F32)
    logit_bound = (1.02 * HEAD_DIM * math.log2(math.e) / math.sqrt(HEAD_DIM)
                   * jnp.max(jnp.abs(q_norm_a[0])) * jnp.max(jnp.abs(k_norm_a[0])))
    direct_ok = logit_bound <= MAX_DIRECT_LOGIT
    y_prompt = _trunk(x_prompt, p, bucket_t, sink_row, rb, direct_ok)
    y_sample = _trunk(x_sample, p, bucket_t, sink_row, rb, direct_ok)
    return (y_prompt, y_sample)
```

```python
import functools
import math

import jax
import jax.numpy as jnp
from jax import lax
from jax.experimental import pallas as pl
from jax.experimental.pallas import tpu as pltpu

D_MODEL = 1024
GRID_W = 64
HEAD_DIM = 64
N_HEADS = 8
N_KV = 2
GROUP = N_HEADS // N_KV
Q_BLOCK = 128
WINDOW = 128
N_BUCKETS = 32
MAX_DISTANCE = 128
ROPE_THETA = 10000.0
AXIS_DIM = HEAD_DIM // 2
ROT = AXIS_DIM // 2
D_FF = 2816
EPS = 1e-6
NEG_INF = -1e30
Q_COLS = N_HEADS * HEAD_DIM
KV_COLS = N_KV * HEAD_DIM

F32 = jnp.float32
BF16 = jnp.bfloat16

V7X_LANES = 128
V7X_VMEM_BYTES = 64 * 1024 * 1024

TOKEN_TILE = 256
KEY_TILE_A = 256
Q_TILE_A = 128
Q_TILE_A_DIRECT = 128
Q_BLOCKS_PER_STEP_B = 4
VT_CHUNK = 128
SUM_ROWS = 16
DIRECT_TILES_PER_TRIP = 16
MAX_DIRECT_LOGIT = 32.0

FM_ROWS = Q_COLS + KV_COLS + KV_COLS + Q_COLS + KV_COLS
TM_COLS = KV_COLS + 2 * D_MODEL


def _resident(shape):
    nd = len(shape)
    return pl.BlockSpec(shape, lambda *_: (0,) * nd, pipeline_mode=pl.Buffered(1))


def _rms(x, g):
    ms = jnp.mean(x * x, axis=-1, keepdims=True)
    return x * lax.rsqrt(ms + EPS) * g


def _sigmoid(x):
    return 1.0 / (1.0 + jnp.exp(-x))


def _swiglu_half_step(x, g_ref, w_in_ref, w_out_ref):
    h = _rms(x, g_ref[...]).astype(BF16)
    ab = jnp.dot(h, w_in_ref[...], preferred_element_type=F32)
    a = ab[:, :D_FF]
    b = ab[:, D_FF:]
    act = (a * _sigmoid(a) * b).astype(BF16)
    return x + 0.5 * jnp.dot(act, w_out_ref[...], preferred_element_type=F32)


def _head_norm_rope_t(z, gain, cs):
    ms = jnp.mean(z * z, axis=0, keepdims=True)
    z = z * lax.rsqrt(ms + EPS) * gain
    cr, sr, cc, sc = cs[0:ROT], cs[ROT:2 * ROT], cs[2 * ROT:3 * ROT], cs[3 * ROT:4 * ROT]
    x1r, x2r = z[0:ROT], z[ROT:2 * ROT]
    x1c, x2c = z[2 * ROT:3 * ROT], z[3 * ROT:4 * ROT]
    return jnp.concatenate(
        [x1r * cr - x2r * sr, x2r * cr + x1r * sr, x1c * cc - x2c * sc, x2c * cc + x1c * sc], axis=0)


def _front_kernel(x_ref, n1_ref, w1i_ref, w1o_ref, nm_ref, wfm_ref, wtm_ref, qn_ref, kn_ref, cs_ref,
                  x1_ref, qa_ref, ka_ref, va_ref, qb_ref, kb_ref, vb_ref, g_ref):
    tm = x_ref.shape[0]
    x1 = _swiglu_half_step(x_ref[...], n1_ref, w1i_ref, w1o_ref)
    x1_ref[...] = x1
    h = _rms(x1, nm_ref[...]).astype(BF16)
    pt = lax.dot_general(wfm_ref[...], h, (((1,), (1,)), ((), ())), preferred_element_type=F32)
    ptm = jnp.dot(h, wtm_ref[...], preferred_element_type=F32)
    cs = cs_ref[...]
    qn = qn_ref[...]
    kn = kn_ref[...]
    scale = math.log2(math.e) / math.sqrt(HEAD_DIM)
    for hd in range(N_HEADS):
        z = pt[hd * HEAD_DIM:(hd + 1) * HEAD_DIM]
        qa_ref[0, hd * HEAD_DIM:(hd + 1) * HEAD_DIM, :] = (_head_norm_rope_t(z, qn, cs) * scale).astype(BF16)
    ka_t = jnp.concatenate(
        [_head_norm_rope_t(pt[Q_COLS + kv * HEAD_DIM:Q_COLS + (kv + 1) * HEAD_DIM], kn, cs) for kv in range(N_KV)],
        axis=0)
    ka_ref[0] = ka_t.T.astype(BF16)
    off = Q_COLS + KV_COLS
    va = pt[off:off + KV_COLS].astype(BF16)
    off += KV_COLS
    qb_ref[0] = (pt[off:off + Q_COLS] * scale).astype(BF16)
    off += Q_COLS
    vb = pt[off:off + KV_COLS].astype(BF16)
    for c in range(tm // VT_CHUNK):
        va_ref[0, c] = va[:, c * VT_CHUNK:(c + 1) * VT_CHUNK]
        vb_ref[0, c] = vb[:, c * VT_CHUNK:(c + 1) * VT_CHUNK]
    kb_ref[0] = ptm[:, :KV_COLS].astype(BF16)
    g_ref[...] = ptm[:, KV_COLS:].astype(BF16)


def _front(x, p, cs, batch, seq):
    tm = TOKEN_TILE
    n = batch * seq
    nt = seq // tm
    grid = (n // tm,)
    tok = lambda i: (i, 0)
    bt3 = lambda i: (i // nt, 0, i % nt)
    out_shape = (
        jax.ShapeDtypeStruct((n, D_MODEL), F32),
        jax.ShapeDtypeStruct((batch, Q_COLS, seq), BF16),
        jax.ShapeDtypeStruct((batch, seq, KV_COLS), BF16),
        jax.ShapeDtypeStruct((batch, seq // VT_CHUNK, KV_COLS, VT_CHUNK), BF16),
        jax.ShapeDtypeStruct((batch, Q_COLS, seq), BF16),
        jax.ShapeDtypeStruct((batch, seq, KV_COLS), BF16),
        jax.ShapeDtypeStruct((batch, seq // VT_CHUNK, KV_COLS, VT_CHUNK), BF16),
        jax.ShapeDtypeStruct((n, 2 * D_MODEL), BF16),
    )
    vt_spec = pl.BlockSpec((1, tm // VT_CHUNK, KV_COLS, VT_CHUNK), lambda i: (i // nt, i % nt, 0, 0))
    out_specs = (
        pl.BlockSpec((tm, D_MODEL), tok),
        pl.BlockSpec((1, Q_COLS, tm), bt3),
        pl.BlockSpec((1, tm, KV_COLS), lambda i: (i // nt, i % nt, 0)),
        vt_spec,
        pl.BlockSpec((1, Q_COLS, tm), bt3),
        pl.BlockSpec((1, tm, KV_COLS), lambda i: (i // nt, i % nt, 0)),
        vt_spec,
        pl.BlockSpec((tm, 2 * D_MODEL), tok),
    )
    in_specs = [
        pl.BlockSpec((tm, D_MODEL), tok),
        _resident((1, D_MODEL)),
        _resident((D_MODEL, 2 * D_FF)),
        _resident((D_FF, D_MODEL)),
        _resident((1, D_MODEL)),
        _resident((FM_ROWS, D_MODEL)),
        _resident((D_MODEL, TM_COLS)),
        _resident((HEAD_DIM, 1)),
        _resident((HEAD_DIM, 1)),
        pl.BlockSpec((4 * ROT, tm), lambda i: (0, i % nt)),
    ]
    return pl.pallas_call(
        _front_kernel,
        grid=grid,
        in_specs=in_specs,
        out_specs=out_specs,
        out_shape=out_shape,
        compiler_params=pltpu.CompilerParams(
            dimension_semantics=("arbitrary",), vmem_limit_bytes=56 * 1024 * 1024),
        name="front",
    )(x, p["n1"], p["w1i"], p["w1o"], p["nm"], p["wfm"], p["wtm"], p["qn"], p["kn"], cs)


def _build_q_weights(q, wq_ref, tq):
    zeros = jnp.zeros((HEAD_DIM, tq), BF16)
    for hd in range(N_HEADS):
        qh = q[hd * HEAD_DIM:(hd + 1) * HEAD_DIM]
        blk = jnp.concatenate([qh, zeros], axis=0) if hd < GROUP else jnp.concatenate([zeros, qh], axis=0)
        wq_ref[:, hd * tq:(hd + 1) * tq] = blk


def _attn_a_online_kernel(q_ref, k_ref, v_ref, o_ref, wq_ref, s0_ref, s1_ref, t0_ref, t1_ref, m_ref, l_ref,
                          acc_ref, *, seq):
    tq = q_ref.shape[2]
    half = GROUP * tq
    tk = KEY_TILE_A
    cpt = tk // VT_CHUNK
    nk = seq // tk
    _build_q_weights(q_ref[0], wq_ref, tq)
    m_ref[...] = jnp.full(m_ref.shape, NEG_INF, F32)
    l_ref[...] = jnp.zeros(l_ref.shape, F32)
    acc_ref[...] = jnp.zeros(acc_ref.shape, F32)

    def scores(j, s_ref, t_ref):
        kt = k_ref[0, pl.ds(pl.multiple_of(j * tk, tk), tk), :]
        s = jnp.dot(kt, wq_ref[...], preferred_element_type=F32)
        s_ref[...] = s
        t_ref[...] = jnp.max(s, axis=0, keepdims=True)

    def softmax_pv(j, s_ref, t_ref):
        m_old = m_ref[...]
        m_new = jnp.maximum(m_old, t_ref[...])
        alpha = jnp.exp2(m_old - m_new)
        p = jnp.exp2(s_ref[...] - m_new)
        l_ref[...] = alpha * l_ref[...] + jnp.sum(p, axis=0, keepdims=True)
        m_ref[...] = m_new
        pb = p.astype(BF16)
        vt = jnp.concatenate([v_ref[0, j * cpt + c] for c in range(cpt)], axis=1)
        for kv in range(N_KV):
            cols = slice(kv * half, (kv + 1) * half)
            pv = jnp.dot(vt[kv * HEAD_DIM:(kv + 1) * HEAD_DIM], pb[:, cols], preferred_element_type=F32)
            acc_ref[kv] = acc_ref[kv] * alpha[:, cols] + pv

    scores(0, s0_ref, t0_ref)

    def body(jj, carry):
        j = 2 * jj
        scores(j + 1, s1_ref, t1_ref)
        softmax_pv(j, s0_ref, t0_ref)
        scores(jnp.minimum(j + 2, nk - 1), s0_ref, t0_ref)
        softmax_pv(j + 1, s1_ref, t1_ref)
        return carry

    lax.fori_loop(0, nk // 2, body, 0)
    inv_l = 1.0 / l_ref[...]
    rows = []
    for hd in range(N_HEADS):
        kv, g = divmod(hd, GROUP)
        cols = slice(g * tq, (g + 1) * tq)
        rows.append(acc_ref[kv][:, cols] * inv_l[:, hd * tq:(hd + 1) * tq])
    o_ref[0] = jnp.concatenate(rows, axis=0).T.astype(BF16)


def _attn_a_online(qt, k, vt, batch, seq):
    tq = Q_TILE_A
    assert (seq // KEY_TILE_A) % 2 == 0
    kernel = functools.partial(_attn_a_online_kernel, seq=seq)
    row = pltpu.VMEM((1, N_HEADS * tq), F32)
    tile = pltpu.VMEM((KEY_TILE_A, N_HEADS * tq), F32)
    return pl.pallas_call(
        kernel,
        grid=(batch, seq // tq),
        in_specs=[
            pl.BlockSpec((1, Q_COLS, tq), lambda b, i: (b, 0, i)),
            pl.BlockSpec((1, seq, KV_COLS), lambda b, i: (b, 0, 0)),
            pl.BlockSpec((1, seq // VT_CHUNK, KV_COLS, VT_CHUNK), lambda b, i: (b, 0, 0, 0)),
        ],
        out_specs=pl.BlockSpec((1, tq, Q_COLS), lambda b, i: (b, i, 0)),
        out_shape=jax.ShapeDtypeStruct((batch, seq, Q_COLS), BF16),
        scratch_shapes=[
            pltpu.VMEM((KV_COLS, N_HEADS * tq), BF16),
            tile, tile,
            row, row,
            row, row,
            pltpu.VMEM((N_KV, HEAD_DIM, GROUP * tq), F32),
        ],
        compiler_params=pltpu.CompilerParams(
            dimension_semantics=("arbitrary", "arbitrary"), vmem_limit_bytes=40 * 1024 * 1024),
        name="attn_a_online",
    )(qt, k, vt)


def _attn_a_direct_kernel(q_ref, k_ref, v_ref, o_ref, wq_ref, p0_ref, p1_ref, acc_ref, *, seq, per_trip):
    tq = q_ref.shape[2]
    half = GROUP * tq
    tk = KEY_TILE_A
    cpt = tk // VT_CHUNK
    nk = seq // tk
    p_refs = (p0_ref, p1_ref)
    _build_q_weights(q_ref[0], wq_ref, tq)
    acc_ref[...] = jnp.zeros(acc_ref.shape, F32)

    def probs(j, p_ref):
        kt = k_ref[0, pl.ds(pl.multiple_of(j * tk, tk), tk), :]
        s = jnp.dot(kt, wq_ref[...], preferred_element_type=F32)
        p_ref[...] = jnp.exp2(s).astype(BF16)

    def pv(j, p_ref):
        vt = jnp.concatenate([v_ref[0, j * cpt + c] for c in range(cpt)], axis=1)
        ones = jnp.ones((SUM_ROWS, tk), BF16)
        for kv in range(N_KV):
            cols = slice(kv * half, (kv + 1) * half)
            lhs = jnp.concatenate([vt[kv * HEAD_DIM:(kv + 1) * HEAD_DIM], ones], axis=0)
            acc_ref[kv] += jnp.dot(lhs, p_ref[:, cols], preferred_element_type=F32)

    def trip(j0, last):
        for u in range(per_trip):
            if not (last and u == per_trip - 1):
                probs(j0 + u + 1, p_refs[(u + 1) % 2])
            pv(j0 + u, p_refs[u % 2])

    probs(0, p0_ref)

    def body(jj, carry):
        trip(per_trip * jj, False)
        return carry

    lax.fori_loop(0, nk // per_trip - 1, body, 0)
    trip(nk - per_trip, True)
    rows = []
    for hd in range(N_HEADS):
        kv, g = divmod(hd, GROUP)
        cols = slice(g * tq, (g + 1) * tq)
        acc = acc_ref[kv]
        rows.append(acc[:HEAD_DIM, cols] * (1.0 / acc[HEAD_DIM:HEAD_DIM + 1, cols]))
    o_ref[0] = jnp.concatenate(rows, axis=0).T.astype(BF16)


def _attn_a_direct(qt, k, vt, batch, seq):
    tq = Q_TILE_A_DIRECT
    nk = seq // KEY_TILE_A
    per_trip = min(DIRECT_TILES_PER_TRIP, nk)
    assert per_trip % 2 == 0 and nk % per_trip == 0
    kernel = functools.partial(_attn_a_direct_kernel, seq=seq, per_trip=per_trip)
    p_tile = pltpu.VMEM((KEY_TILE_A, N_HEADS * tq), BF16)
    return pl.pallas_call(
        kernel,
        grid=(batch, seq // tq),
        in_specs=[
            pl.BlockSpec((1, Q_COLS, tq), lambda b, i: (b, 0, i)),
            pl.BlockSpec((1, seq, KV_COLS), lambda b, i: (b, 0, 0)),
            pl.BlockSpec((1, seq // VT_CHUNK, KV_COLS, VT_CHUNK), lambda b, i: (b, 0, 0, 0)),
        ],
        out_specs=pl.BlockSpec((1, tq, Q_COLS), lambda b, i: (b, i, 0)),
        out_shape=jax.ShapeDtypeStruct((batch, seq, Q_COLS), BF16),
        scratch_shapes=[
            pltpu.VMEM((KV_COLS, N_HEADS * tq), BF16),
            p_tile, p_tile,
            pltpu.VMEM((N_KV, HEAD_DIM + SUM_ROWS, GROUP * tq), F32),
        ],
        compiler_params=pltpu.CompilerParams(
            dimension_semantics=("arbitrary", "arbitrary"), vmem_limit_bytes=40 * 1024 * 1024),
        name="attn_a_direct",
    )(qt, k, vt)


def _attn_a(qt, k, vt, direct_ok, batch, seq):
    return lax.cond(direct_ok,
                    lambda: _attn_a_direct(qt, k, vt, batch, seq),
                    lambda: _attn_a_online(qt, k, vt, batch, seq))


def _attn_b_kernel(rb_ref, q_ref, k_ref, v_ref, bucket_ref, sink_ref, o_ref, wq_ref, bias_ref, *, seq):
    tq = Q_BLOCK
    nb = seq // tq
    half = GROUP * tq
    log2e = math.log2(math.e)
    step = pl.program_id(1)

    @pl.when((pl.program_id(0) == 0) & (step == 0))
    def _():
        for jb in range(3):
            bucket = bucket_ref[jb]
            key = lax.broadcasted_iota(jnp.int32, (tq, tq), 0)
            qry = lax.broadcasted_iota(jnp.int32, (tq, tq), 1)
            rel = (jb - 1) * tq + key - qry
            inside = jnp.abs(rel) <= WINDOW
            for hd in range(N_HEADS):
                tbl = jnp.zeros((tq, tq), F32)
                for b in range(N_BUCKETS):
                    tbl = jnp.where(bucket == b, rb_ref[b, hd], tbl)
                bias_ref[jb, :, hd * tq:(hd + 1) * tq] = jnp.where(inside, tbl * log2e, NEG_INF)
        bias_ref[3] = jnp.full(bias_ref.shape[1:], NEG_INF, F32)

    sink = sink_ref[...] * log2e
    ones = jnp.ones((SUM_ROWS, tq), BF16)
    logits = []
    for sub in range(Q_BLOCKS_PER_STEP_B):
        i = step * Q_BLOCKS_PER_STEP_B + sub
        wq = wq_ref.at[sub]
        _build_q_weights(q_ref[0, :, sub * tq:(sub + 1) * tq], wq, tq)
        parts = []
        idxs = []
        for jb in range(3):
            blk = i - 1 + jb
            valid = (blk >= 0) & (blk < nb)
            idx = jnp.clip(blk, 0, nb - 1)
            kt = k_ref[0, pl.ds(pl.multiple_of(idx * tq, tq), tq), :]
            tbl = bias_ref[jnp.where(valid, jb, 3)]
            parts.append(jnp.dot(kt, wq[...], preferred_element_type=F32) + tbl)
            idxs.append(idx)
        logits.append((jnp.concatenate(parts, axis=0), idxs))
    for sub, (s, idxs) in enumerate(logits):
        m = jnp.maximum(jnp.max(s, axis=0, keepdims=True), sink)
        pb = jnp.exp2(s - m).astype(BF16)
        p_sink = jnp.exp2(sink - m)
        rows = []
        for kv in range(N_KV):
            cols = slice(kv * half, (kv + 1) * half)
            acc = jnp.zeros((HEAD_DIM + SUM_ROWS, half), F32)
            for jb in range(3):
                lhs = jnp.concatenate([v_ref[0, idxs[jb]][kv * HEAD_DIM:(kv + 1) * HEAD_DIM], ones], axis=0)
                acc = acc + jnp.dot(lhs, pb[jb * tq:(jb + 1) * tq, cols], preferred_element_type=F32)
            out = acc[:HEAD_DIM] * (1.0 / (acc[HEAD_DIM:HEAD_DIM + 1] + p_sink[:, cols]))
            for g in range(GROUP):
                rows.append(out[:, g * tq:(g + 1) * tq])
        o_ref[0, sub * tq:(sub + 1) * tq, :] = jnp.concatenate(rows, axis=0).T.astype(BF16)


def _attn_b(qt, k, vt, rel_bias, bucket_t, sink_row, batch, seq):
    tq = Q_BLOCK
    tqs = Q_BLOCKS_PER_STEP_B * tq
    assert seq % tqs == 0
    kernel = functools.partial(_attn_b_kernel, seq=seq)
    return pl.pallas_call(
        kernel,
        grid=(batch, seq // tqs),
        in_specs=[
            pl.BlockSpec(memory_space=pltpu.SMEM),
            pl.BlockSpec((1, Q_COLS, tqs), lambda b, i: (b, 0, i)),
            pl.BlockSpec((1, seq, KV_COLS), lambda b, i: (b, 0, 0)),
            pl.BlockSpec((1, seq // VT_CHUNK, KV_COLS, VT_CHUNK), lambda b, i: (b, 0, 0, 0)),
            pl.BlockSpec((3, tq, tq), lambda b, i: (0, 0, 0)),
            pl.BlockSpec((1, N_HEADS * tq), lambda b, i: (0, 0)),
        ],
        out_specs=pl.BlockSpec((1, tqs, Q_COLS), lambda b, i: (b, i, 0)),
        out_shape=jax.ShapeDtypeStruct((batch, seq, Q_COLS), BF16),
        scratch_shapes=[
            pltpu.VMEM((Q_BLOCKS_PER_STEP_B, KV_COLS, N_HEADS * tq), BF16),
            pltpu.VMEM((4, tq, N_HEADS * tq), F32),
        ],
        compiler_params=pltpu.CompilerParams(
            dimension_semantics=("arbitrary", "arbitrary"), vmem_limit_bytes=40 * 1024 * 1024),
        name="attn_b",
    )(rel_bias, qt, k, vt, bucket_t, sink_row)


def _back_kernel(x1_ref, oa_ref, ob_ref, g_ref, wba_ref, wbb_ref, wo_ref, n2_ref, w2i_ref, w2o_ref, nf_ref, y_ref):
    ya = jnp.dot(oa_ref[...], wba_ref[...], preferred_element_type=F32)
    yb = jnp.dot(ob_ref[...], wbb_ref[...], preferred_element_type=F32)
    g = g_ref[...].astype(F32)
    merged = _sigmoid(g[:, :D_MODEL]) * ya + _sigmoid(g[:, D_MODEL:]) * yb
    x2 = x1_ref[...] + jnp.dot(merged.astype(BF16), wo_ref[...], preferred_element_type=F32)
    x3 = _swiglu_half_step(x2, n2_ref, w2i_ref, w2o_ref)
    y_ref[...] = _rms(x3, nf_ref[...])


def _back(x1, oa, ob, g, p):
    tm = TOKEN_TILE
    n = x1.shape[0]
    tok = lambda i: (i, 0)
    return pl.pallas_call(
        _back_kernel,
        grid=(n // tm,),
        in_specs=[
            pl.BlockSpec((tm, D_MODEL), tok),
            pl.BlockSpec((tm, Q_COLS), tok),
            pl.BlockSpec((tm, Q_COLS), tok),
            pl.BlockSpec((tm, 2 * D_MODEL), tok),
            _resident((Q_COLS, D_MODEL)),
            _resident((Q_COLS, D_MODEL)),
            _resident((D_MODEL, D_MODEL)),
            _resident((1, D_MODEL)),
            _resident((D_MODEL, 2 * D_FF)),
            _resident((D_FF, D_MODEL)),
            _resident((1, D_MODEL)),
        ],
        out_specs=pl.BlockSpec((tm, D_MODEL), tok),
        out_shape=jax.ShapeDtypeStruct((n, D_MODEL), F32),
        compiler_params=pltpu.CompilerParams(
            dimension_semantics=("arbitrary",), vmem_limit_bytes=56 * 1024 * 1024),
        name="back",
    )(x1, oa, ob, g, p["wba"], p["wbb"], p["wo"], p["n2"], p["w2i"], p["w2o"], p["nf"])


def _t5_bucket(rel):
    nb = N_BUCKETS // 2
    max_exact = nb // 2
    ret = jnp.where(rel > 0, nb, 0)
    n = jnp.abs(rel)
    large = max_exact + (jnp.log(jnp.maximum(n, 1).astype(F32) / max_exact)
                         / math.log(MAX_DISTANCE / max_exact) * (nb - max_exact)).astype(jnp.int32)
    large = jnp.minimum(large, nb - 1)
    return ret + jnp.where(n < max_exact, n, large)


def _rope_table(seq):
    rows = seq // GRID_W
    grid_r, grid_c = jnp.meshgrid(jnp.arange(rows, dtype=F32), jnp.arange(GRID_W, dtype=F32), indexing="ij")
    pos = jnp.stack([grid_r.reshape(-1), grid_c.reshape(-1)], axis=-1)
    inv = ROPE_THETA ** (-jnp.arange(0, AXIS_DIM, 2, dtype=F32) / AXIS_DIM)
    ang = pos[:, :, None] * inv
    cos = jnp.cos(ang)
    sin = jnp.sin(ang)
    return jnp.concatenate([cos[:, 0].T, sin[:, 0].T, cos[:, 1].T, sin[:, 1].T], axis=0)


def _trunk(x, p, bucket_t, sink_row, rel_bias, direct_ok):
    batch, seq, _ = x.shape
    cs = _rope_table(seq)
    x1, qa, ka, va, qb, kb, vb, g = _front(x.reshape(batch * seq, D_MODEL), p, cs, batch, seq)
    oa = _attn_a(qa, ka, va, direct_ok, batch, seq)
    ob = _attn_b(qb, kb, vb, rel_bias, bucket_t, sink_row, batch, seq)
    y = _back(x1, oa.reshape(batch * seq, Q_COLS), ob.reshape(batch * seq, Q_COLS), g, p)
    return y.reshape(batch, seq, D_MODEL)


def kernel(x_prompt, x_sample, norm_ffn1, w_ffn1_in, w_ffn1_out, norm_mix, w_in, q_norm_a, k_norm_a, sink_b,
           w_branch_a, w_branch_b, w_out, norm_ffn2, w_ffn2_in, w_ffn2_out, rel_bias, norm_final):
    assert norm_ffn1.shape[0] == 1, "single-layer trunk"
    wi = w_in[0]
    c = 0
    parts = {}
    for name, width in (("qa", Q_COLS), ("ka", KV_COLS), ("va", KV_COLS), ("qb", Q_COLS), ("kb", KV_COLS),
                        ("vb", KV_COLS), ("ga", D_MODEL), ("gb", D_MODEL)):
        parts[name] = wi[:, c:c + width]
        c += width
    wfm = jnp.concatenate([parts[k] for k in ("qa", "ka", "va", "qb", "vb")], axis=1).T.astype(BF16)
    wtm = jnp.concatenate([parts[k] for k in ("kb", "ga", "gb")], axis=1).astype(BF16)
    p = dict(
        n1=norm_ffn1[0].reshape(1, D_MODEL), w1i=w_ffn1_in[0].astype(BF16), w1o=w_ffn1_out[0].astype(BF16),
        nm=norm_mix[0].reshape(1, D_MODEL), wfm=wfm, wtm=wtm,
        qn=q_norm_a[0].reshape(HEAD_DIM, 1), kn=k_norm_a[0].reshape(HEAD_DIM, 1),
        wba=w_branch_a[0].astype(BF16), wbb=w_branch_b[0].astype(BF16), wo=w_out[0].astype(BF16),
        n2=norm_ffn2[0].reshape(1, D_MODEL), w2i=w_ffn2_in[0].astype(BF16), w2o=w_ffn2_out[0].astype(BF16),
        nf=norm_final.reshape(1, D_MODEL),
    )
    key = jnp.arange(3 * Q_BLOCK)[:, None] - Q_BLOCK
    qry = jnp.arange(Q_BLOCK)[None, :]
    bucket_t = _t5_bucket(key - qry).astype(jnp.int32).reshape(3, Q_BLOCK, Q_BLOCK)
    sink_row = jnp.repeat(sink_b[0].astype(F32), Q_BLOCK).reshape(1, N_HEADS * Q_BLOCK)
    rb = rel_bias.astype(F32)
    logit_bound = (1.02 * HEAD_DIM * math.log2(math.e) / math.sqrt(HEAD_DIM)
                   * jnp.max(jnp.abs(q_norm_a[0])) * jnp.max(jnp.abs(k_norm_a[0])))
    direct_ok = logit_bound <= MAX_DIRECT_LOGIT
    y_prompt = _trunk(x_prompt, p, bucket_t, sink_row, rb, direct_ok)
    y_sample = _trunk(x_sample, p, bucket_t, sink_row, rb, direct_ok)
    return (y_prompt, y_sample)
```

```python
import functools
import math

import jax
import jax.numpy as jnp
from jax import lax
from jax.experimental import pallas as pl
from jax.experimental.pallas import tpu as pltpu

D_MODEL = 1024
GRID_W = 64
HEAD_DIM = 64
N_HEADS = 8
N_KV = 2
GROUP = N_HEADS // N_KV
Q_BLOCK = 128
WINDOW = 128
N_BUCKETS = 32
MAX_DISTANCE = 128
ROPE_THETA = 10000.0
AXIS_DIM = HEAD_DIM // 2
ROT = AXIS_DIM // 2
D_FF = 2816
EPS = 1e-6
NEG_INF = -1e30
Q_COLS = N_HEADS * HEAD_DIM
KV_COLS = N_KV * HEAD_DIM

F32 = jnp.float32
BF16 = jnp.bfloat16

V7X_LANES = 128
V7X_VMEM_BYTES = 64 * 1024 * 1024

TOKEN_TILE = 256
KEY_TILE_A = 256
Q_TILE_A = 128
Q_TILE_A_DIRECT = 128
Q_TILES_PER_STEP_A = 2
Q_BLOCKS_PER_STEP_B = 4
VT_CHUNK = 128
SUM_ROWS = 16
DIRECT_TILES_PER_TRIP = 16
MAX_DIRECT_LOGIT = 32.0

FM_ROWS = Q_COLS + KV_COLS + KV_COLS + Q_COLS + KV_COLS
TM_COLS = KV_COLS + 2 * D_MODEL


def _resident(shape):
    nd = len(shape)
    return pl.BlockSpec(shape, lambda *_: (0,) * nd, pipeline_mode=pl.Buffered(1))


def _rms(x, g):
    ms = jnp.mean(x * x, axis=-1, keepdims=True)
    return x * lax.rsqrt(ms + EPS) * g


def _sigmoid(x):
    return 1.0 / (1.0 + jnp.exp(-x))


def _prenorm_operand(x, g):
    r = lax.rsqrt(jnp.mean(x * x, axis=-1, keepdims=True) + EPS)
    return (x * g).astype(BF16), r


def _swiglu_half_step(x, g_ref, w_in_ref, w_out_ref):
    xg, r = _prenorm_operand(x, g_ref[...])
    ab = jnp.dot(xg, w_in_ref[...], preferred_element_type=F32) * r
    a = ab[:, :D_FF]
    b = ab[:, D_FF:]
    act = (a * _sigmoid(a) * b).astype(BF16)
    return x + 0.5 * jnp.dot(act, w_out_ref[...], preferred_element_type=F32)


def _head_norm_rope_t(z, gain, cs):
    ms = jnp.mean(z * z, axis=0, keepdims=True)
    z = z * lax.rsqrt(ms + EPS) * gain
    cr, sr, cc, sc = cs[0:ROT], cs[ROT:2 * ROT], cs[2 * ROT:3 * ROT], cs[3 * ROT:4 * ROT]
    x1r, x2r = z[0:ROT], z[ROT:2 * ROT]
    x1c, x2c = z[2 * ROT:3 * ROT], z[3 * ROT:4 * ROT]
    return jnp.concatenate(
        [x1r * cr - x2r * sr, x2r * cr + x1r * sr, x1c * cc - x2c * sc, x2c * cc + x1c * sc], axis=0)


def _front_kernel(x_ref, n1_ref, w1i_ref, w1o_ref, nm_ref, wfm_ref, wtm_ref, qn_ref, kn_ref, cs_ref,
                  x1_ref, qa_ref, ka_ref, va_ref, qb_ref, kb_ref, vb_ref, g_ref):
    tm = x_ref.shape[0]
    x1 = _swiglu_half_step(x_ref[...], n1_ref, w1i_ref, w1o_ref)
    x1_ref[...] = x1
    h, r = _prenorm_operand(x1, nm_ref[...])
    r_t = jnp.broadcast_to(r, (tm, V7X_LANES)).T[:1]
    pt = lax.dot_general(wfm_ref[...], h, (((1,), (1,)), ((), ())), preferred_element_type=F32) * r_t
    ptm = jnp.dot(h, wtm_ref[...], preferred_element_type=F32) * r
    cs = cs_ref[...]
    qn = qn_ref[...]
    kn = kn_ref[...]
    scale = math.log2(math.e) / math.sqrt(HEAD_DIM)
    for hd in range(N_HEADS):
        z = pt[hd * HEAD_DIM:(hd + 1) * HEAD_DIM]
        qa_ref[0, hd * HEAD_DIM:(hd + 1) * HEAD_DIM, :] = (_head_norm_rope_t(z, qn, cs) * scale).astype(BF16)
    ka_t = jnp.concatenate(
        [_head_norm_rope_t(pt[Q_COLS + kv * HEAD_DIM:Q_COLS + (kv + 1) * HEAD_DIM], kn, cs) for kv in range(N_KV)],
        axis=0)
    ka_ref[0] = ka_t.T.astype(BF16)
    off = Q_COLS + KV_COLS
    va = pt[off:off + KV_COLS].astype(BF16)
    off += KV_COLS
    qb_ref[0] = (pt[off:off + Q_COLS] * scale).astype(BF16)
    off += Q_COLS
    vb = pt[off:off + KV_COLS].astype(BF16)
    for c in range(tm // VT_CHUNK):
        va_ref[0, c] = va[:, c * VT_CHUNK:(c + 1) * VT_CHUNK]
        vb_ref[0, c] = vb[:, c * VT_CHUNK:(c + 1) * VT_CHUNK]
    kb_ref[0] = ptm[:, :KV_COLS].astype(BF16)
    g_ref[...] = ptm[:, KV_COLS:].astype(BF16)


def _front(x, p, cs, batch, seq):
    tm = TOKEN_TILE
    n = batch * seq
    nt = seq // tm
    grid = (n // tm,)
    tok = lambda i: (i, 0)
    bt3 = lambda i: (i // nt, 0, i % nt)
    out_shape = (
        jax.ShapeDtypeStruct((n, D_MODEL), F32),
        jax.ShapeDtypeStruct((batch, Q_COLS, seq), BF16),
        jax.ShapeDtypeStruct((batch, seq, KV_COLS), BF16),
        jax.ShapeDtypeStruct((batch, seq // VT_CHUNK, KV_COLS, VT_CHUNK), BF16),
        jax.ShapeDtypeStruct((batch, Q_COLS, seq), BF16),
        jax.ShapeDtypeStruct((batch, seq, KV_COLS), BF16),
        jax.ShapeDtypeStruct((batch, seq // VT_CHUNK, KV_COLS, VT_CHUNK), BF16),
        jax.ShapeDtypeStruct((n, 2 * D_MODEL), BF16),
    )
    vt_spec = pl.BlockSpec((1, tm // VT_CHUNK, KV_COLS, VT_CHUNK), lambda i: (i // nt, i % nt, 0, 0))
    out_specs = (
        pl.BlockSpec((tm, D_MODEL), tok),
        pl.BlockSpec((1, Q_COLS, tm), bt3),
        pl.BlockSpec((1, tm, KV_COLS), lambda i: (i // nt, i % nt, 0)),
        vt_spec,
        pl.BlockSpec((1, Q_COLS, tm), bt3),
        pl.BlockSpec((1, tm, KV_COLS), lambda i: (i // nt, i % nt, 0)),
        vt_spec,
        pl.BlockSpec((tm, 2 * D_MODEL), tok),
    )
    in_specs = [
        pl.BlockSpec((tm, D_MODEL), tok),
        _resident((1, D_MODEL)),
        _resident((D_MODEL, 2 * D_FF)),
        _resident((D_FF, D_MODEL)),
        _resident((1, D_MODEL)),
        _resident((FM_ROWS, D_MODEL)),
        _resident((D_MODEL, TM_COLS)),
        _resident((HEAD_DIM, 1)),
        _resident((HEAD_DIM, 1)),
        pl.BlockSpec((4 * ROT, tm), lambda i: (0, i % nt)),
    ]
    return pl.pallas_call(
        _front_kernel,
        grid=grid,
        in_specs=in_specs,
        out_specs=out_specs,
        out_shape=out_shape,
        compiler_params=pltpu.CompilerParams(
            dimension_semantics=("arbitrary",), vmem_limit_bytes=56 * 1024 * 1024),
        name="front",
    )(x, p["n1"], p["w1i"], p["w1o"], p["nm"], p["wfm"], p["wtm"], p["qn"], p["kn"], cs)


def _build_q_weights(q, wq_ref, tq):
    zeros = jnp.zeros((HEAD_DIM, tq), BF16)
    for hd in range(N_HEADS):
        qh = q[hd * HEAD_DIM:(hd + 1) * HEAD_DIM]
        blk = jnp.concatenate([qh, zeros], axis=0) if hd < GROUP else jnp.concatenate([zeros, qh], axis=0)
        wq_ref[:, hd * tq:(hd + 1) * tq] = blk


def _attn_a_online_kernel(q_ref, k_ref, v_ref, o_ref, wq_ref, s0_ref, s1_ref, t0_ref, t1_ref, m_ref, l_ref,
                          acc_ref, *, seq):
    tq = q_ref.shape[2]
    half = GROUP * tq
    tk = KEY_TILE_A
    cpt = tk // VT_CHUNK
    nk = seq // tk
    _build_q_weights(q_ref[0], wq_ref, tq)
    m_ref[...] = jnp.full(m_ref.shape, NEG_INF, F32)
    l_ref[...] = jnp.zeros(l_ref.shape, F32)
    acc_ref[...] = jnp.zeros(acc_ref.shape, F32)

    def scores(j, s_ref, t_ref):
        kt = k_ref[0, pl.ds(pl.multiple_of(j * tk, tk), tk), :]
        s = jnp.dot(kt, wq_ref[...], preferred_element_type=F32)
        s_ref[...] = s
        t_ref[...] = jnp.max(s, axis=0, keepdims=True)

    def softmax_pv(j, s_ref, t_ref):
        m_old = m_ref[...]
        m_new = jnp.maximum(m_old, t_ref[...])
        alpha = jnp.exp2(m_old - m_new)
        p = jnp.exp2(s_ref[...] - m_new)
        l_ref[...] = alpha * l_ref[...] + jnp.sum(p, axis=0, keepdims=True)
        m_ref[...] = m_new
        pb = p.astype(BF16)
        vt = jnp.concatenate([v_ref[0, j * cpt + c] for c in range(cpt)], axis=1)
        for kv in range(N_KV):
            cols = slice(kv * half, (kv + 1) * half)
            pv = jnp.dot(vt[kv * HEAD_DIM:(kv + 1) * HEAD_DIM], pb[:, cols], preferred_element_type=F32)
            acc_ref[kv] = acc_ref[kv] * alpha[:, cols] + pv

    scores(0, s0_ref, t0_ref)

    def body(jj, carry):
        j = 2 * jj
        scores(j + 1, s1_ref, t1_ref)
        softmax_pv(j, s0_ref, t0_ref)
        scores(jnp.minimum(j + 2, nk - 1), s0_ref, t0_ref)
        softmax_pv(j + 1, s1_ref, t1_ref)
        return carry

    lax.fori_loop(0, nk // 2, body, 0)
    inv_l = 1.0 / l_ref[...]
    rows = []
    for hd in range(N_HEADS):
        kv, g = divmod(hd, GROUP)
        cols = slice(g * tq, (g + 1) * tq)
        rows.append(acc_ref[kv][:, cols] * inv_l[:, hd * tq:(hd + 1) * tq])
    o_ref[0] = jnp.concatenate(rows, axis=0).T.astype(BF16)


def _attn_a_online(qt, k, vt, batch, seq):
    tq = Q_TILE_A
    assert (seq // KEY_TILE_A) % 2 == 0
    kernel = functools.partial(_attn_a_online_kernel, seq=seq)
    row = pltpu.VMEM((1, N_HEADS * tq), F32)
    tile = pltpu.VMEM((KEY_TILE_A, N_HEADS * tq), F32)
    return pl.pallas_call(
        kernel,
        grid=(batch, seq // tq),
        in_specs=[
            pl.BlockSpec((1, Q_COLS, tq), lambda b, i: (b, 0, i)),
            pl.BlockSpec((1, seq, KV_COLS), lambda b, i: (b, 0, 0)),
            pl.BlockSpec((1, seq // VT_CHUNK, KV_COLS, VT_CHUNK), lambda b, i: (b, 0, 0, 0)),
        ],
        out_specs=pl.BlockSpec((1, tq, Q_COLS), lambda b, i: (b, i, 0)),
        out_shape=jax.ShapeDtypeStruct((batch, seq, Q_COLS), BF16),
        scratch_shapes=[
            pltpu.VMEM((KV_COLS, N_HEADS * tq), BF16),
            tile, tile,
            row, row,
            row, row,
            pltpu.VMEM((N_KV, HEAD_DIM, GROUP * tq), F32),
        ],
        compiler_params=pltpu.CompilerParams(
            dimension_semantics=("arbitrary", "arbitrary"), vmem_limit_bytes=40 * 1024 * 1024),
        name="attn_a_online",
    )(qt, k, vt)


def _attn_a_direct_kernel(q_ref, k_ref, v_ref, o_ref, wq_ref, p0_ref, p1_ref, acc_ref, *, seq, per_trip):
    tq = Q_TILE_A_DIRECT
    half = GROUP * tq
    tk = KEY_TILE_A
    cpt = tk // VT_CHUNK
    nk = seq // tk
    p_refs = (p0_ref, p1_ref)

    def probs(sub, j, p_ref):
        kt = k_ref[0, pl.ds(pl.multiple_of(j * tk, tk), tk), :]
        s = jnp.dot(kt, wq_ref[sub], preferred_element_type=F32)
        p_ref[...] = jnp.exp2(s).astype(BF16)

    def pv(sub, j, p_ref):
        vt = jnp.concatenate([v_ref[0, j * cpt + c] for c in range(cpt)], axis=1)
        ones = jnp.ones((SUM_ROWS, tk), BF16)
        for kv in range(N_KV):
            cols = slice(kv * half, (kv + 1) * half)
            lhs = jnp.concatenate([vt[kv * HEAD_DIM:(kv + 1) * HEAD_DIM], ones], axis=0)
            acc_ref[sub, kv] += jnp.dot(lhs, p_ref[:, cols], preferred_element_type=F32)

    def start(sub):
        _build_q_weights(q_ref[0, :, sub * tq:(sub + 1) * tq], wq_ref.at[sub], tq)
        acc_ref[sub] = jnp.zeros(acc_ref.shape[1:], F32)
        probs(sub, 0, p0_ref)

    def finish(sub):
        rows = []
        for hd in range(N_HEADS):
            kv, g = divmod(hd, GROUP)
            cols = slice(g * tq, (g + 1) * tq)
            acc = acc_ref[sub, kv]
            rows.append(acc[:HEAD_DIM, cols] * (1.0 / acc[HEAD_DIM:HEAD_DIM + 1, cols]))
        o_ref[0, sub * tq:(sub + 1) * tq, :] = jnp.concatenate(rows, axis=0).T.astype(BF16)

    start(0)
    for sub in range(Q_TILES_PER_STEP_A):

        def body(jj, carry, sub=sub):
            for u in range(per_trip):
                probs(sub, per_trip * jj + u + 1, p_refs[(u + 1) % 2])
                pv(sub, per_trip * jj + u, p_refs[u % 2])
            return carry

        lax.fori_loop(0, nk // per_trip - 1, body, 0)
        j0 = nk - per_trip
        for u in range(per_trip):
            if u < per_trip - 1:
                probs(sub, j0 + u + 1, p_refs[(u + 1) % 2])
            elif sub + 1 < Q_TILES_PER_STEP_A:
                start(sub + 1)
            pv(sub, j0 + u, p_refs[u % 2])
        finish(sub)


def _attn_a_direct(qt, k, vt, batch, seq):
    tq = Q_TILE_A_DIRECT
    tqs = Q_TILES_PER_STEP_A * tq
    nk = seq // KEY_TILE_A
    per_trip = min(DIRECT_TILES_PER_TRIP, nk)
    assert per_trip % 2 == 0 and nk % per_trip == 0 and seq % tqs == 0
    kernel = functools.partial(_attn_a_direct_kernel, seq=seq, per_trip=per_trip)
    p_tile = pltpu.VMEM((KEY_TILE_A, N_HEADS * tq), BF16)
    return pl.pallas_call(
        kernel,
        grid=(batch, seq // tqs),
        in_specs=[
            pl.BlockSpec((1, Q_COLS, tqs), lambda b, i: (b, 0, i)),
            pl.BlockSpec((1, seq, KV_COLS), lambda b, i: (b, 0, 0)),
            pl.BlockSpec((1, seq // VT_CHUNK, KV_COLS, VT_CHUNK), lambda b, i: (b, 0, 0, 0)),
        ],
        out_specs=pl.BlockSpec((1, tqs, Q_COLS), lambda b, i: (b, i, 0)),
        out_shape=jax.ShapeDtypeStruct((batch, seq, Q_COLS), BF16),
        scratch_shapes=[
            pltpu.VMEM((Q_TILES_PER_STEP_A, KV_COLS, N_HEADS * tq), BF16),
            p_tile, p_tile,
            pltpu.VMEM((Q_TILES_PER_STEP_A, N_KV, HEAD_DIM + SUM_ROWS, GROUP * tq), F32),
        ],
        compiler_params=pltpu.CompilerParams(
            dimension_semantics=("arbitrary", "arbitrary"), vmem_limit_bytes=40 * 1024 * 1024),
        name="attn_a_direct",
    )(qt, k, vt)


def _attn_a(qt, k, vt, direct_ok, batch, seq):
    return lax.cond(direct_ok,
                    lambda: _attn_a_direct(qt, k, vt, batch, seq),
                    lambda: _attn_a_online(qt, k, vt, batch, seq))


def _attn_b_kernel(rb_ref, q_ref, k_ref, v_ref, bucket_ref, sink_ref, o_ref, wq_ref, bias_ref, *, seq):
    tq = Q_BLOCK
    nb = seq // tq
    half = GROUP * tq
    log2e = math.log2(math.e)
    step = pl.program_id(1)

    @pl.when((pl.program_id(0) == 0) & (step == 0))
    def _():
        for jb in range(3):
            bucket = bucket_ref[jb]
            key = lax.broadcasted_iota(jnp.int32, (tq, tq), 0)
            qry = lax.broadcasted_iota(jnp.int32, (tq, tq), 1)
            rel = (jb - 1) * tq + key - qry
            inside = jnp.abs(rel) <= WINDOW
            for hd in range(N_HEADS):
                tbl = jnp.zeros((tq, tq), F32)
                for b in range(N_BUCKETS):
                    tbl = jnp.where(bucket == b, rb_ref[b, hd], tbl)
                bias_ref[jb, :, hd * tq:(hd + 1) * tq] = jnp.where(inside, tbl * log2e, NEG_INF)
        bias_ref[3] = jnp.full(bias_ref.shape[1:], NEG_INF, F32)

    sink = sink_ref[...] * log2e
    ones = jnp.ones((SUM_ROWS, tq), BF16)
    logits = []
    for sub in range(Q_BLOCKS_PER_STEP_B):
        i = step * Q_BLOCKS_PER_STEP_B + sub
        wq = wq_ref.at[sub]
        _build_q_weights(q_ref[0, :, sub * tq:(sub + 1) * tq], wq, tq)
        parts = []
        idxs = []
        for jb in range(3):
            blk = i - 1 + jb
            valid = (blk >= 0) & (blk < nb)
            idx = jnp.clip(blk, 0, nb - 1)
            kt = k_ref[0, pl.ds(pl.multiple_of(idx * tq, tq), tq), :]
            tbl = bias_ref[jnp.where(valid, jb, 3)]
            parts.append(jnp.dot(kt, wq[...], preferred_element_type=F32) + tbl)
            idxs.append(idx)
        logits.append((jnp.concatenate(parts, axis=0), idxs))
    for sub, (s, idxs) in enumerate(logits):
        m = jnp.maximum(jnp.max(s, axis=0, keepdims=True), sink)
        pb = jnp.exp2(s - m).astype(BF16)
        p_sink = jnp.exp2(sink - m)
        rows = []
        for kv in range(N_KV):
            cols = slice(kv * half, (kv + 1) * half)
            acc = jnp.zeros((HEAD_DIM + SUM_ROWS, half), F32)
            for jb in range(3):
                lhs = jnp.concatenate([v_ref[0, idxs[jb]][kv * HEAD_DIM:(kv + 1) * HEAD_DIM], ones], axis=0)
                acc = acc + jnp.dot(lhs, pb[jb * tq:(jb + 1) * tq, cols], preferred_element_type=F32)
            out = acc[:HEAD_DIM] * (1.0 / (acc[HEAD_DIM:HEAD_DIM + 1] + p_sink[:, cols]))
            for g in range(GROUP):
                rows.append(out[:, g * tq:(g + 1) * tq])
        o_ref[0, sub * tq:(sub + 1) * tq, :] = jnp.concatenate(rows, axis=0).T.astype(BF16)


def _attn_b(qt, k, vt, rel_bias, bucket_t, sink_row, batch, seq):
    tq = Q_BLOCK
    tqs = Q_BLOCKS_PER_STEP_B * tq
    assert seq % tqs == 0
    kernel = functools.partial(_attn_b_kernel, seq=seq)
    return pl.pallas_call(
        kernel,
        grid=(batch, seq // tqs),
        in_specs=[
            pl.BlockSpec(memory_space=pltpu.SMEM),
            pl.BlockSpec((1, Q_COLS, tqs), lambda b, i: (b, 0, i)),
            pl.BlockSpec((1, seq, KV_COLS), lambda b, i: (b, 0, 0)),
            pl.BlockSpec((1, seq // VT_CHUNK, KV_COLS, VT_CHUNK), lambda b, i: (b, 0, 0, 0)),
            pl.BlockSpec((3, tq, tq), lambda b, i: (0, 0, 0)),
            pl.BlockSpec((1, N_HEADS * tq), lambda b, i: (0, 0)),
        ],
        out_specs=pl.BlockSpec((1, tqs, Q_COLS), lambda b, i: (b, i, 0)),
        out_shape=jax.ShapeDtypeStruct((batch, seq, Q_COLS), BF16),
        scratch_shapes=[
            pltpu.VMEM((Q_BLOCKS_PER_STEP_B, KV_COLS, N_HEADS * tq), BF16),
            pltpu.VMEM((4, tq, N_HEADS * tq), F32),
        ],
        compiler_params=pltpu.CompilerParams(
            dimension_semantics=("arbitrary", "arbitrary"), vmem_limit_bytes=40 * 1024 * 1024),
        name="attn_b",
    )(rel_bias, qt, k, vt, bucket_t, sink_row)


def _back_kernel(x1_ref, oa_ref, ob_ref, g_ref, wba_ref, wbb_ref, wo_ref, n2_ref, w2i_ref, w2o_ref, nf_ref, y_ref):
    ya = jnp.dot(oa_ref[...], wba_ref[...], preferred_element_type=F32)
    yb = jnp.dot(ob_ref[...], wbb_ref[...], preferred_element_type=F32)
    g = g_ref[...].astype(F32)
    merged = _sigmoid(g[:, :D_MODEL]) * ya + _sigmoid(g[:, D_MODEL:]) * yb
    x2 = x1_ref[...] + jnp.dot(merged.astype(BF16), wo_ref[...], preferred_element_type=F32)
    x3 = _swiglu_half_step(x2, n2_ref, w2i_ref, w2o_ref)
    y_ref[...] = _rms(x3, nf_ref[...])


def _back(x1, oa, ob, g, p):
    tm = TOKEN_TILE
    n = x1.shape[0]
    tok = lambda i: (i, 0)
    return pl.pallas_call(
        _back_kernel,
        grid=(n // tm,),
        in_specs=[
            pl.BlockSpec((tm, D_MODEL), tok),
            pl.BlockSpec((tm, Q_COLS), tok),
            pl.BlockSpec((tm, Q_COLS), tok),
            pl.BlockSpec((tm, 2 * D_MODEL), tok),
            _resident((Q_COLS, D_MODEL)),
            _resident((Q_COLS, D_MODEL)),
            _resident((D_MODEL, D_MODEL)),
            _resident((1, D_MODEL)),
            _resident((D_MODEL, 2 * D_FF)),
            _resident((D_FF, D_MODEL)),
            _resident((1, D_MODEL)),
        ],
        out_specs=pl.BlockSpec((tm, D_MODEL), tok),
        out_shape=jax.ShapeDtypeStruct((n, D_MODEL), F32),
        compiler_params=pltpu.CompilerParams(
            dimension_semantics=("arbitrary",), vmem_limit_bytes=56 * 1024 * 1024),
        name="back",
    )(x1, oa, ob, g, p["wba"], p["wbb"], p["wo"], p["n2"], p["w2i"], p["w2o"], p["nf"])


def _t5_bucket(rel):
    nb = N_BUCKETS // 2
    max_exact = nb // 2
    ret = jnp.where(rel > 0, nb, 0)
    n = jnp.abs(rel)
    large = max_exact + (jnp.log(jnp.maximum(n, 1).astype(F32) / max_exact)
                         / math.log(MAX_DISTANCE / max_exact) * (nb - max_exact)).astype(jnp.int32)
    large = jnp.minimum(large, nb - 1)
    return ret + jnp.where(n < max_exact, n, large)


def _rope_table(seq):
    rows = seq // GRID_W
    grid_r, grid_c = jnp.meshgrid(jnp.arange(rows, dtype=F32), jnp.arange(GRID_W, dtype=F32), indexing="ij")
    pos = jnp.stack([grid_r.reshape(-1), grid_c.reshape(-1)], axis=-1)
    inv = ROPE_THETA ** (-jnp.arange(0, AXIS_DIM, 2, dtype=F32) / AXIS_DIM)
    ang = pos[:, :, None] * inv
    cos = jnp.cos(ang)
    sin = jnp.sin(ang)
    return jnp.concatenate([cos[:, 0].T, sin[:, 0].T, cos[:, 1].T, sin[:, 1].T], axis=0)


def _trunk(x, p, bucket_t, sink_row, rel_bias, direct_ok):
    batch, seq, _ = x.shape
    cs = _rope_table(seq)
    x1, qa, ka, va, qb, kb, vb, g = _front(x.reshape(batch * seq, D_MODEL), p, cs, batch, seq)
    oa = _attn_a(qa, ka, va, direct_ok, batch, seq)
    ob = _attn_b(qb, kb, vb, rel_bias, bucket_t, sink_row, batch, seq)
    y = _back(x1, oa.reshape(batch * seq, Q_COLS), ob.reshape(batch * seq, Q_COLS), g, p)
    return y.reshape(batch, seq, D_MODEL)


def kernel(x_prompt, x_sample, norm_ffn1, w_ffn1_in, w_ffn1_out, norm_mix, w_in, q_norm_a, k_norm_a, sink_b,
           w_branch_a, w_branch_b, w_out, norm_ffn2, w_ffn2_in, w_ffn2_out, rel_bias, norm_final):
    assert norm_ffn1.shape[0] == 1, "single-layer trunk"
    wi = w_in[0]
    c = 0
    parts = {}
    for name, width in (("qa", Q_COLS), ("ka", KV_COLS), ("va", KV_COLS), ("qb", Q_COLS), ("kb", KV_COLS),
                        ("vb", KV_COLS), ("ga", D_MODEL), ("gb", D_MODEL)):
        parts[name] = wi[:, c:c + width]
        c += width
    wfm = jnp.concatenate([parts[k] for k in ("qa", "ka", "va", "qb", "vb")], axis=1).T.astype(BF16)
    wtm = jnp.concatenate([parts[k] for k in ("kb", "ga", "gb")], axis=1).astype(BF16)
    p = dict(
        n1=norm_ffn1[0].reshape(1, D_MODEL), w1i=w_ffn1_in[0].astype(BF16), w1o=w_ffn1_out[0].astype(BF16),
        nm=norm_mix[0].reshape(1, D_MODEL), wfm=wfm, wtm=wtm,
        qn=q_norm_a[0].reshape(HEAD_DIM, 1), kn=k_norm_a[0].reshape(HEAD_DIM, 1),
        wba=w_branch_a[0].astype(BF16), wbb=w_branch_b[0].astype(BF16), wo=w_out[0].astype(BF16),
        n2=norm_ffn2[0].reshape(1, D_MODEL), w2i=w_ffn2_in[0].astype(BF16), w2o=w_ffn2_out[0].astype(BF16),
        nf=norm_final.reshape(1, D_MODEL),
    )
    key = jnp.arange(3 * Q_BLOCK)[:, None] - Q_BLOCK
    qry = jnp.arange(Q_BLOCK)[None, :]
    bucket_t = _t5_bucket(key - qry).astype(jnp.int32).reshape(3, Q_BLOCK, Q_BLOCK)
    sink_row = jnp.repeat(sink_b[0].astype(F32), Q_BLOCK).reshape(1, N_HEADS * Q_BLOCK)
    rb = rel_bias.astype(F32)
    logit_bound = (1.02 * HEAD_DIM * math.log2(math.e) / math.sqrt(HEAD_DIM)
                   * jnp.max(jnp.abs(q_norm_a[0])) * jnp.max(jnp.abs(k_norm_a[0])))
    direct_ok = logit_bound <= MAX_DIRECT_LOGIT
    y_prompt = _trunk(x_prompt, p, bucket_t, sink_row, rb, direct_ok)
    y_sample = _trunk(x_sample, p, bucket_t, sink_row, rb, direct_ok)
    return (y_prompt, y_sample)
```

```python
import functools
import math

import jax
import jax.numpy as jnp
from jax import lax
from jax.experimental import pallas as pl
from jax.experimental.pallas import tpu as pltpu

D_MODEL = 1024
GRID_W = 64
HEAD_DIM = 64
N_HEADS = 8
N_KV = 2
GROUP = N_HEADS // N_KV
Q_BLOCK = 128
WINDOW = 128
N_BUCKETS = 32
MAX_DISTANCE = 128
ROPE_THETA = 10000.0
AXIS_DIM = HEAD_DIM // 2
ROT = AXIS_DIM // 2
D_FF = 2816
EPS = 1e-6
NEG_INF = -1e30
Q_COLS = N_HEADS * HEAD_DIM
KV_COLS = N_KV * HEAD_DIM

F32 = jnp.float32
BF16 = jnp.bfloat16

V7X_LANES = 128
V7X_VMEM_BYTES = 64 * 1024 * 1024

TOKEN_TILE = 256
KEY_TILE_A = 256
KEY_TILE_A_DIRECT = 256
Q_TILE_A = 128
Q_TILE_A_DIRECT = 128
Q_TILES_PER_STEP_A = 2
Q_BLOCKS_PER_STEP_B = 8
VT_CHUNK = 128
SUM_ROWS = 16
DIRECT_TILES_PER_TRIP = 16
MAX_DIRECT_LOGIT = 32.0

FM_ROWS = Q_COLS + KV_COLS + KV_COLS + Q_COLS + KV_COLS
TM_COLS = KV_COLS + 2 * D_MODEL


def _resident(shape):
    nd = len(shape)
    return pl.BlockSpec(shape, lambda *_: (0,) * nd, pipeline_mode=pl.Buffered(1))


def _rms(x, g):
    ms = jnp.mean(x * x, axis=-1, keepdims=True)
    return x * lax.rsqrt(ms + EPS) * g


def _sigmoid(x):
    return 0.5 * jnp.tanh(0.5 * x) + 0.5


def _prenorm_operand(x, g):
    r = lax.rsqrt(jnp.mean(x * x, axis=-1, keepdims=True) + EPS)
    return (x * g).astype(BF16), r


def _swiglu_half_step(x, g_ref, w_in_ref, w_out_ref):
    xg, r = _prenorm_operand(x, g_ref[...])
    ab = jnp.dot(xg, w_in_ref[...], preferred_element_type=F32)
    ah = ab[:, :D_FF] * (0.5 * r)
    b = ab[:, D_FF:] * r
    act = (ah * (1.0 + jnp.tanh(ah)) * b).astype(BF16)
    return x + 0.5 * jnp.dot(act, w_out_ref[...], preferred_element_type=F32)


def _head_norm_rope_t(z, gain, cs):
    ms = jnp.mean(z * z, axis=0, keepdims=True)
    z = z * lax.rsqrt(ms + EPS) * gain
    cr, sr, cc, sc = cs[0:ROT], cs[ROT:2 * ROT], cs[2 * ROT:3 * ROT], cs[3 * ROT:4 * ROT]
    x1r, x2r = z[0:ROT], z[ROT:2 * ROT]
    x1c, x2c = z[2 * ROT:3 * ROT], z[3 * ROT:4 * ROT]
    return jnp.concatenate(
        [x1r * cr - x2r * sr, x2r * cr + x1r * sr, x1c * cc - x2c * sc, x2c * cc + x1c * sc], axis=0)


def _front_kernel(x_ref, n1_ref, w1i_ref, w1o_ref, nm_ref, wfm_ref, wtm_ref, qn_ref, kn_ref, cs_ref,
                  x1_ref, qa_ref, ka_ref, va_ref, qb_ref, kb_ref, vb_ref, g_ref):
    tm = x_ref.shape[0]
    x1 = _swiglu_half_step(x_ref[...], n1_ref, w1i_ref, w1o_ref)
    x1_ref[...] = x1
    h, r = _prenorm_operand(x1, nm_ref[...])
    r_t = jnp.broadcast_to(r, (tm, V7X_LANES)).T[:1]
    pt = lax.dot_general(wfm_ref[...], h, (((1,), (1,)), ((), ())), preferred_element_type=F32) * r_t
    ptm = jnp.dot(h, wtm_ref[...], preferred_element_type=F32) * r
    cs = cs_ref[...]
    qn = qn_ref[...]
    kn = kn_ref[...]
    scale = math.log2(math.e) / math.sqrt(HEAD_DIM)
    for hd in range(N_HEADS):
        z = pt[hd * HEAD_DIM:(hd + 1) * HEAD_DIM]
        qa_ref[0, hd * HEAD_DIM:(hd + 1) * HEAD_DIM, :] = (_head_norm_rope_t(z, qn, cs) * scale).astype(BF16)
    ka_t = jnp.concatenate(
        [_head_norm_rope_t(pt[Q_COLS + kv * HEAD_DIM:Q_COLS + (kv + 1) * HEAD_DIM], kn, cs) for kv in range(N_KV)],
        axis=0)
    ka_ref[0] = ka_t.T.astype(BF16)
    off = Q_COLS + KV_COLS
    va = pt[off:off + KV_COLS].astype(BF16)
    off += KV_COLS
    qb_ref[0] = (pt[off:off + Q_COLS] * scale).astype(BF16)
    off += Q_COLS
    vb = pt[off:off + KV_COLS].astype(BF16)
    for c in range(tm // VT_CHUNK):
        va_ref[0, c] = va[:, c * VT_CHUNK:(c + 1) * VT_CHUNK]
        vb_ref[0, c] = vb[:, c * VT_CHUNK:(c + 1) * VT_CHUNK]
    kb_ref[0] = ptm[:, :KV_COLS].astype(BF16)
    g_ref[...] = ptm[:, KV_COLS:].astype(BF16)


def _front(x, p, cs, batch, seq):
    tm = TOKEN_TILE
    n = batch * seq
    nt = seq // tm
    grid = (n // tm,)
    tok = lambda i: (i, 0)
    bt3 = lambda i: (i // nt, 0, i % nt)
    out_shape = (
        jax.ShapeDtypeStruct((n, D_MODEL), F32),
        jax.ShapeDtypeStruct((batch, Q_COLS, seq), BF16),
        jax.ShapeDtypeStruct((batch, seq, KV_COLS), BF16),
        jax.ShapeDtypeStruct((batch, seq // VT_CHUNK, KV_COLS, VT_CHUNK), BF16),
        jax.ShapeDtypeStruct((batch, Q_COLS, seq), BF16),
        jax.ShapeDtypeStruct((batch, seq, KV_COLS), BF16),
        jax.ShapeDtypeStruct((batch, seq // VT_CHUNK, KV_COLS, VT_CHUNK), BF16),
        jax.ShapeDtypeStruct((n, 2 * D_MODEL), BF16),
    )
    vt_spec = pl.BlockSpec((1, tm // VT_CHUNK, KV_COLS, VT_CHUNK), lambda i: (i // nt, i % nt, 0, 0))
    out_specs = (
        pl.BlockSpec((tm, D_MODEL), tok),
        pl.BlockSpec((1, Q_COLS, tm), bt3),
        pl.BlockSpec((1, tm, KV_COLS), lambda i: (i // nt, i % nt, 0)),
        vt_spec,
        pl.BlockSpec((1, Q_COLS, tm), bt3),
        pl.BlockSpec((1, tm, KV_COLS), lambda i: (i // nt, i % nt, 0)),
        vt_spec,
        pl.BlockSpec((tm, 2 * D_MODEL), tok),
    )
    in_specs = [
        pl.BlockSpec((tm, D_MODEL), tok),
        _resident((1, D_MODEL)),
        _resident((D_MODEL, 2 * D_FF)),
        _resident((D_FF, D_MODEL)),
        _resident((1, D_MODEL)),
        _resident((FM_ROWS, D_MODEL)),
        _resident((D_MODEL, TM_COLS)),
        _resident((HEAD_DIM, 1)),
        _resident((HEAD_DIM, 1)),
        pl.BlockSpec((4 * ROT, tm), lambda i: (0, i % nt)),
    ]
    return pl.pallas_call(
        _front_kernel,
        grid=grid,
        in_specs=in_specs,
        out_specs=out_specs,
        out_shape=out_shape,
        compiler_params=pltpu.CompilerParams(
            dimension_semantics=("arbitrary",), vmem_limit_bytes=56 * 1024 * 1024),
        name="front",
    )(x, p["n1"], p["w1i"], p["w1o"], p["nm"], p["wfm"], p["wtm"], p["qn"], p["kn"], cs)


def _build_q_weights(q, wq_ref, tq):
    zeros = jnp.zeros((HEAD_DIM, tq), BF16)
    for hd in range(N_HEADS):
        qh = q[hd * HEAD_DIM:(hd + 1) * HEAD_DIM]
        blk = jnp.concatenate([qh, zeros], axis=0) if hd < GROUP else jnp.concatenate([zeros, qh], axis=0)
        wq_ref[:, hd * tq:(hd + 1) * tq] = blk


def _attn_a_online_kernel(q_ref, k_ref, v_ref, o_ref, wq_ref, s0_ref, s1_ref, t0_ref, t1_ref, m_ref, l_ref,
                          acc_ref, *, seq):
    tq = q_ref.shape[2]
    half = GROUP * tq
    tk = KEY_TILE_A
    cpt = tk // VT_CHUNK
    nk = seq // tk
    _build_q_weights(q_ref[0], wq_ref, tq)
    m_ref[...] = jnp.full(m_ref.shape, NEG_INF, F32)
    l_ref[...] = jnp.zeros(l_ref.shape, F32)
    acc_ref[...] = jnp.zeros(acc_ref.shape, F32)

    def scores(j, s_ref, t_ref):
        kt = k_ref[0, pl.ds(pl.multiple_of(j * tk, tk), tk), :]
        s = jnp.dot(kt, wq_ref[...], preferred_element_type=F32)
        s_ref[...] = s
        t_ref[...] = jnp.max(s, axis=0, keepdims=True)

    def softmax_pv(j, s_ref, t_ref):
        m_old = m_ref[...]
        m_new = jnp.maximum(m_old, t_ref[...])
        alpha = jnp.exp2(m_old - m_new)
        p = jnp.exp2(s_ref[...] - m_new)
        l_ref[...] = alpha * l_ref[...] + jnp.sum(p, axis=0, keepdims=True)
        m_ref[...] = m_new
        pb = p.astype(BF16)
        vt = jnp.concatenate([v_ref[0, j * cpt + c] for c in range(cpt)], axis=1)
        for kv in range(N_KV):
            cols = slice(kv * half, (kv + 1) * half)
            pv = jnp.dot(vt[kv * HEAD_DIM:(kv + 1) * HEAD_DIM], pb[:, cols], preferred_element_type=F32)
            acc_ref[kv] = acc_ref[kv] * alpha[:, cols] + pv

    scores(0, s0_ref, t0_ref)

    def body(jj, carry):
        j = 2 * jj
        scores(j + 1, s1_ref, t1_ref)
        softmax_pv(j, s0_ref, t0_ref)
        scores(jnp.minimum(j + 2, nk - 1), s0_ref, t0_ref)
        softmax_pv(j + 1, s1_ref, t1_ref)
        return carry

    lax.fori_loop(0, nk // 2, body, 0)
    inv_l = 1.0 / l_ref[...]
    rows = []
    for hd in range(N_HEADS):
        kv, g = divmod(hd, GROUP)
        cols = slice(g * tq, (g + 1) * tq)
        rows.append(acc_ref[kv][:, cols] * inv_l[:, hd * tq:(hd + 1) * tq])
    o_ref[0] = jnp.concatenate(rows, axis=0).T.astype(BF16)


def _attn_a_online(qt, k, vt, batch, seq):
    tq = Q_TILE_A
    assert (seq // KEY_TILE_A) % 2 == 0
    kernel = functools.partial(_attn_a_online_kernel, seq=seq)
    row = pltpu.VMEM((1, N_HEADS * tq), F32)
    tile = pltpu.VMEM((KEY_TILE_A, N_HEADS * tq), F32)
    return pl.pallas_call(
        kernel,
        grid=(batch, seq // tq),
        in_specs=[
            pl.BlockSpec((1, Q_COLS, tq), lambda b, i: (b, 0, i)),
            pl.BlockSpec((1, seq, KV_COLS), lambda b, i: (b, 0, 0)),
            pl.BlockSpec((1, seq // VT_CHUNK, KV_COLS, VT_CHUNK), lambda b, i: (b, 0, 0, 0)),
        ],
        out_specs=pl.BlockSpec((1, tq, Q_COLS), lambda b, i: (b, i, 0)),
        out_shape=jax.ShapeDtypeStruct((batch, seq, Q_COLS), BF16),
        scratch_shapes=[
            pltpu.VMEM((KV_COLS, N_HEADS * tq), BF16),
            tile, tile,
            row, row,
            row, row,
            pltpu.VMEM((N_KV, HEAD_DIM, GROUP * tq), F32),
        ],
        compiler_params=pltpu.CompilerParams(
            dimension_semantics=("arbitrary", "arbitrary"), vmem_limit_bytes=40 * 1024 * 1024),
        name="attn_a_online",
    )(qt, k, vt)


def _attn_a_direct_kernel(q_ref, k_ref, v_ref, o_ref, wq_ref, p0_ref, p1_ref, l_ref, acc_ref, *, seq, per_trip):
    tq = Q_TILE_A_DIRECT
    half = GROUP * tq
    tk = KEY_TILE_A_DIRECT
    cpt = tk // VT_CHUNK
    nk = seq // tk
    p_refs = (p0_ref, p1_ref)

    def probs(sub, j, p_ref):
        kt = k_ref[0, pl.ds(pl.multiple_of(j * tk, tk), tk), :]
        s = jnp.dot(kt, wq_ref[sub], preferred_element_type=F32)
        p = jnp.exp2(s)
        l_ref[sub] += jnp.sum(p, axis=0, keepdims=True)
        p_ref[...] = p.astype(BF16)

    def pv(sub, j, p_ref):
        vt = jnp.concatenate([v_ref[0, j * cpt + c] for c in range(cpt)], axis=1)
        for kv in range(N_KV):
            cols = slice(kv * half, (kv + 1) * half)
            acc_ref[sub, kv] += jnp.dot(vt[kv * HEAD_DIM:(kv + 1) * HEAD_DIM], p_ref[:, cols],
                                        preferred_element_type=F32)

    def start(sub):
        _build_q_weights(q_ref[0, :, sub * tq:(sub + 1) * tq], wq_ref.at[sub], tq)
        acc_ref[sub] = jnp.zeros(acc_ref.shape[1:], F32)
        l_ref[sub] = jnp.zeros(l_ref.shape[1:], F32)
        probs(sub, 0, p0_ref)

    def finish(sub):
        inv_l = 1.0 / l_ref[sub]
        rows = []
        for hd in range(N_HEADS):
            kv, g = divmod(hd, GROUP)
            cols = slice(g * tq, (g + 1) * tq)
            rows.append(acc_ref[sub, kv][:, cols] * inv_l[:, hd * tq:(hd + 1) * tq])
        o_ref[0, sub * tq:(sub + 1) * tq, :] = jnp.concatenate(rows, axis=0).T.astype(BF16)

    start(0)
    for sub in range(Q_TILES_PER_STEP_A):

        def body(jj, carry, sub=sub):
            for u in range(per_trip):
                probs(sub, per_trip * jj + u + 1, p_refs[(u + 1) % 2])
                pv(sub, per_trip * jj + u, p_refs[u % 2])
            return carry

        lax.fori_loop(0, nk // per_trip - 1, body, 0)
        j0 = nk - per_trip
        for u in range(per_trip):
            if u < per_trip - 1:
                probs(sub, j0 + u + 1, p_refs[(u + 1) % 2])
            elif sub + 1 < Q_TILES_PER_STEP_A:
                start(sub + 1)
            pv(sub, j0 + u, p_refs[u % 2])
        finish(sub)


def _attn_a_direct(qt, k, vt, batch, seq):
    tq = Q_TILE_A_DIRECT
    tqs = Q_TILES_PER_STEP_A * tq
    nk = seq // KEY_TILE_A_DIRECT
    per_trip = min(DIRECT_TILES_PER_TRIP, nk)
    assert per_trip % 2 == 0 and nk % per_trip == 0 and seq % tqs == 0
    kernel = functools.partial(_attn_a_direct_kernel, seq=seq, per_trip=per_trip)
    p_tile = pltpu.VMEM((KEY_TILE_A_DIRECT, N_HEADS * tq), BF16)
    return pl.pallas_call(
        kernel,
        grid=(batch, seq // tqs),
        in_specs=[
            pl.BlockSpec((1, Q_COLS, tqs), lambda b, i: (b, 0, i)),
            pl.BlockSpec((1, seq, KV_COLS), lambda b, i: (b, 0, 0)),
            pl.BlockSpec((1, seq // VT_CHUNK, KV_COLS, VT_CHUNK), lambda b, i: (b, 0, 0, 0)),
        ],
        out_specs=pl.BlockSpec((1, tqs, Q_COLS), lambda b, i: (b, i, 0)),
        out_shape=jax.ShapeDtypeStruct((batch, seq, Q_COLS), BF16),
        scratch_shapes=[
            pltpu.VMEM((Q_TILES_PER_STEP_A, KV_COLS, N_HEADS * tq), BF16),
            p_tile, p_tile,
            pltpu.VMEM((Q_TILES_PER_STEP_A, 1, N_HEADS * tq), F32),
            pltpu.VMEM((Q_TILES_PER_STEP_A, N_KV, HEAD_DIM, GROUP * tq), F32),
        ],
        compiler_params=pltpu.CompilerParams(
            dimension_semantics=("arbitrary", "arbitrary"), vmem_limit_bytes=40 * 1024 * 1024),
        name="attn_a_direct",
    )(qt, k, vt)


def _attn_a(qt, k, vt, direct_ok, batch, seq):
    return lax.cond(direct_ok,
                    lambda: _attn_a_direct(qt, k, vt, batch, seq),
                    lambda: _attn_a_online(qt, k, vt, batch, seq))


def _attn_b_kernel(rb_ref, q_ref, k_ref, v_ref, bucket_ref, sink_ref, o_ref, wq_ref, bias_ref, *, seq):
    step = pl.program_id(1)

    @pl.when((pl.program_id(0) == 0) & (step == 0))
    def _():
        _window_bias_tables(rb_ref, bucket_ref, bias_ref)

    sink = sink_ref[...] * math.log2(math.e)
    logits = []
    for sub in range(Q_BLOCKS_PER_STEP_B):
        wq = wq_ref.at[sub]
        _build_q_weights(q_ref[0, :, sub * Q_BLOCK:(sub + 1) * Q_BLOCK], wq, Q_BLOCK)
        logits.append(_window_logits(step * Q_BLOCKS_PER_STEP_B + sub, wq, k_ref, bias_ref, seq))
    for sub, (s, idxs) in enumerate(logits):
        o_ref[0, sub * Q_BLOCK:(sub + 1) * Q_BLOCK, :] = _window_softmax_pv(s, idxs, sink, v_ref)


def _window_bias_tables(rb_ref, bucket_ref, bias_ref):
    tq = Q_BLOCK
    log2e = math.log2(math.e)
    for jb in range(3):
        bucket = bucket_ref[jb]
        key = lax.broadcasted_iota(jnp.int32, (tq, tq), 0)
        qry = lax.broadcasted_iota(jnp.int32, (tq, tq), 1)
        rel = (jb - 1) * tq + key - qry
        inside = jnp.abs(rel) <= WINDOW
        for hd in range(N_HEADS):
            tbl = jnp.zeros((tq, tq), F32)
            for b in range(N_BUCKETS):
                tbl = jnp.where(bucket == b, rb_ref[b, hd], tbl)
            bias_ref[jb, :, hd * tq:(hd + 1) * tq] = jnp.where(inside, tbl * log2e, NEG_INF)
    bias_ref[3] = jnp.full(bias_ref.shape[1:], NEG_INF, F32)


def _window_logits(i, wq, k_ref, bias_ref, seq):
    tq = Q_BLOCK
    nb = seq // tq
    parts = []
    idxs = []
    for jb in range(3):
        blk = i - 1 + jb
        valid = (blk >= 0) & (blk < nb)
        idx = jnp.clip(blk, 0, nb - 1)
        kt = k_ref[0, pl.ds(pl.multiple_of(idx * tq, tq), tq), :]
        tbl = bias_ref[jnp.where(valid, jb, 3)]
        parts.append(jnp.dot(kt, wq[...], preferred_element_type=F32) + tbl)
        idxs.append(idx)
    return jnp.concatenate(parts, axis=0), idxs


def _window_softmax_pv(s, idxs, sink, v_ref):
    tq = Q_BLOCK
    half = GROUP * tq
    ones = jnp.ones((SUM_ROWS, tq), BF16)
    m = jnp.maximum(jnp.max(s, axis=0, keepdims=True), sink)
    pb = jnp.exp2(s - m).astype(BF16)
    p_sink = jnp.exp2(sink - m)
    rows = []
    for kv in range(N_KV):
        cols = slice(kv * half, (kv + 1) * half)
        acc = jnp.zeros((HEAD_DIM + SUM_ROWS, half), F32)
        for jb in range(3):
            lhs = jnp.concatenate([v_ref[0, idxs[jb]][kv * HEAD_DIM:(kv + 1) * HEAD_DIM], ones], axis=0)
            acc = acc + jnp.dot(lhs, pb[jb * tq:(jb + 1) * tq, cols], preferred_element_type=F32)
        out = acc[:HEAD_DIM] * (1.0 / (acc[HEAD_DIM:HEAD_DIM + 1] + p_sink[:, cols]))
        for g in range(GROUP):
            rows.append(out[:, g * tq:(g + 1) * tq])
    return jnp.concatenate(rows, axis=0).T.astype(BF16)


def _attn_b(qt, k, vt, rel_bias, bucket_t, sink_row, batch, seq):
    tq = Q_BLOCK
    tqs = Q_BLOCKS_PER_STEP_B * tq
    assert seq % tqs == 0
    kernel = functools.partial(_attn_b_kernel, seq=seq)
    return pl.pallas_call(
        kernel,
        grid=(batch, seq // tqs),
        in_specs=[
            pl.BlockSpec(memory_space=pltpu.SMEM),
            pl.BlockSpec((1, Q_COLS, tqs), lambda b, i: (b, 0, i)),
            pl.BlockSpec((1, seq, KV_COLS), lambda b, i: (b, 0, 0)),
            pl.BlockSpec((1, seq // VT_CHUNK, KV_COLS, VT_CHUNK), lambda b, i: (b, 0, 0, 0)),
            pl.BlockSpec((3, tq, tq), lambda b, i: (0, 0, 0)),
            pl.BlockSpec((1, N_HEADS * tq), lambda b, i: (0, 0)),
        ],
        out_specs=pl.BlockSpec((1, tqs, Q_COLS), lambda b, i: (b, i, 0)),
        out_shape=jax.ShapeDtypeStruct((batch, seq, Q_COLS), BF16),
        scratch_shapes=[
            pltpu.VMEM((Q_BLOCKS_PER_STEP_B, KV_COLS, N_HEADS * tq), BF16),
            pltpu.VMEM((4, tq, N_HEADS * tq), F32),
        ],
        compiler_params=pltpu.CompilerParams(
            dimension_semantics=("arbitrary", "arbitrary"), vmem_limit_bytes=40 * 1024 * 1024),
        name="attn_b",
    )(rel_bias, qt, k, vt, bucket_t, sink_row)


def _back_kernel(x1_ref, oa_ref, ob_ref, g_ref, wba_ref, wbb_ref, wo_ref, n2_ref, w2i_ref, w2o_ref, nf_ref, y_ref):
    ya = jnp.dot(oa_ref[...], wba_ref[...], preferred_element_type=F32)
    yb = jnp.dot(ob_ref[...], wbb_ref[...], preferred_element_type=F32)
    g = g_ref[...].astype(F32)
    merged = _sigmoid(g[:, :D_MODEL]) * ya + _sigmoid(g[:, D_MODEL:]) * yb
    x2 = x1_ref[...] + jnp.dot(merged.astype(BF16), wo_ref[...], preferred_element_type=F32)
    x3 = _swiglu_half_step(x2, n2_ref, w2i_ref, w2o_ref)
    y_ref[...] = _rms(x3, nf_ref[...])


def _back(x1, oa, ob, g, p):
    tm = TOKEN_TILE
    n = x1.shape[0]
    tok = lambda i: (i, 0)
    return pl.pallas_call(
        _back_kernel,
        grid=(n // tm,),
        in_specs=[
            pl.BlockSpec((tm, D_MODEL), tok),
            pl.BlockSpec((tm, Q_COLS), tok),
            pl.BlockSpec((tm, Q_COLS), tok),
            pl.BlockSpec((tm, 2 * D_MODEL), tok),
            _resident((Q_COLS, D_MODEL)),
            _resident((Q_COLS, D_MODEL)),
            _resident((D_MODEL, D_MODEL)),
            _resident((1, D_MODEL)),
            _resident((D_MODEL, 2 * D_FF)),
            _resident((D_FF, D_MODEL)),
            _resident((1, D_MODEL)),
        ],
        out_specs=pl.BlockSpec((tm, D_MODEL), tok),
        out_shape=jax.ShapeDtypeStruct((n, D_MODEL), F32),
        compiler_params=pltpu.CompilerParams(
            dimension_semantics=("arbitrary",), vmem_limit_bytes=56 * 1024 * 1024),
        name="back",
    )(x1, oa, ob, g, p["wba"], p["wbb"], p["wo"], p["n2"], p["w2i"], p["w2o"], p["nf"])


def _t5_bucket(rel):
    nb = N_BUCKETS // 2
    max_exact = nb // 2
    ret = jnp.where(rel > 0, nb, 0)
    n = jnp.abs(rel)
    large = max_exact + (jnp.log(jnp.maximum(n, 1).astype(F32) / max_exact)
                         / math.log(MAX_DISTANCE / max_exact) * (nb - max_exact)).astype(jnp.int32)
    large = jnp.minimum(large, nb - 1)
    return ret + jnp.where(n < max_exact, n, large)


def _rope_table(seq):
    rows = seq // GRID_W
    grid_r, grid_c = jnp.meshgrid(jnp.arange(rows, dtype=F32), jnp.arange(GRID_W, dtype=F32), indexing="ij")
    pos = jnp.stack([grid_r.reshape(-1), grid_c.reshape(-1)], axis=-1)
    inv = ROPE_THETA ** (-jnp.arange(0, AXIS_DIM, 2, dtype=F32) / AXIS_DIM)
    ang = pos[:, :, None] * inv
    cos = jnp.cos(ang)
    sin = jnp.sin(ang)
    return jnp.concatenate([cos[:, 0].T, sin[:, 0].T, cos[:, 1].T, sin[:, 1].T], axis=0)


def _trunk(x, p, bucket_t, sink_row, rel_bias, direct_ok):
    batch, seq, _ = x.shape
    cs = _rope_table(seq)
    x1, qa, ka, va, qb, kb, vb, g = _front(x.reshape(batch * seq, D_MODEL), p, cs, batch, seq)
    oa = _attn_a(qa, ka, va, direct_ok, batch, seq)
    ob = _attn_b(qb, kb, vb, rel_bias, bucket_t, sink_row, batch, seq)
    y = _back(x1, oa.reshape(batch * seq, Q_COLS), ob.reshape(batch * seq, Q_COLS), g, p)
    return y.reshape(batch, seq, D_MODEL)


def kernel(x_prompt, x_sample, norm_ffn1, w_ffn1_in, w_ffn1_out, norm_mix, w_in, q_norm_a, k_norm_a, sink_b,
           w_branch_a, w_branch_b, w_out, norm_ffn2, w_ffn2_in, w_ffn2_out, rel_bias, norm_final):
    assert norm_ffn1.shape[0] == 1, "single-layer trunk"
    wi = w_in[0]
    c = 0
    parts = {}
    for name, width in (("qa", Q_COLS), ("ka", KV_COLS), ("va", KV_COLS), ("qb", Q_COLS), ("kb", KV_COLS),
                        ("vb", KV_COLS), ("ga", D_MODEL), ("gb", D_MODEL)):
        parts[name] = wi[:, c:c + width]
        c += width
    wfm = jnp.concatenate([parts[k] for k in ("qa", "ka", "va", "qb", "vb")], axis=1).T.astype(BF16)
    wtm = jnp.concatenate([parts[k] for k in ("kb", "ga", "gb")], axis=1).astype(BF16)
    p = dict(
        n1=norm_ffn1[0].reshape(1, D_MODEL), w1i=w_ffn1_in[0].astype(BF16), w1o=w_ffn1_out[0].astype(BF16),
        nm=norm_mix[0].reshape(1, D_MODEL), wfm=wfm, wtm=wtm,
        qn=q_norm_a[0].reshape(HEAD_DIM, 1), kn=k_norm_a[0].reshape(HEAD_DIM, 1),
        wba=w_branch_a[0].astype(BF16), wbb=w_branch_b[0].astype(BF16), wo=w_out[0].astype(BF16),
        n2=norm_ffn2[0].reshape(1, D_MODEL), w2i=w_ffn2_in[0].astype(BF16), w2o=w_ffn2_out[0].astype(BF16),
        nf=norm_final.reshape(1, D_MODEL),
    )
    key = jnp.arange(3 * Q_BLOCK)[:, None] - Q_BLOCK
    qry = jnp.arange(Q_BLOCK)[None, :]
    bucket_t = _t5_bucket(key - qry).astype(jnp.int32).reshape(3, Q_BLOCK, Q_BLOCK)
    sink_row = jnp.repeat(sink_b[0].astype(F32), Q_BLOCK).reshape(1, N_HEADS * Q_BLOCK)
    rb = rel_bias.astype(F32)
    logit_bound = (1.02 * HEAD_DIM * math.log2(math.e) / math.sqrt(HEAD_DIM)
                   * jnp.max(jnp.abs(q_norm_a[0])) * jnp.max(jnp.abs(k_norm_a[0])))
    direct_ok = logit_bound <= MAX_DIRECT_LOGIT
    y_prompt = _trunk(x_prompt, p, bucket_t, sink_row, rb, direct_ok)
    y_sample = _trunk(x_sample, p, bucket_t, sink_row, rb, direct_ok)
    return (y_prompt, y_sample)
```

```python
import functools
import math

import jax
import jax.numpy as jnp
from jax import lax
from jax.experimental import pallas as pl
from jax.experimental.pallas import tpu as pltpu

D_MODEL = 1024
GRID_W = 64
HEAD_DIM = 64
N_HEADS = 8
N_KV = 2
GROUP = N_HEADS // N_KV
Q_BLOCK = 128
WINDOW = 128
N_BUCKETS = 32
MAX_DISTANCE = 128
ROPE_THETA = 10000.0
AXIS_DIM = HEAD_DIM // 2
ROT = AXIS_DIM // 2
D_FF = 2816
EPS = 1e-6
NEG_INF = -1e30
Q_COLS = N_HEADS * HEAD_DIM
KV_COLS = N_KV * HEAD_DIM

F32 = jnp.float32
BF16 = jnp.bfloat16

V7X_LANES = 128
V7X_VMEM_BYTES = 64 * 1024 * 1024
MIB = 1024 * 1024
TOKEN_KERNEL_VMEM_LIMIT = V7X_VMEM_BYTES - 8 * MIB
ATTN_VMEM_LIMIT = 40 * MIB

TOKEN_TILE = 256
KEY_TILE_A = 256
KEY_TILE_A_DIRECT = 256
Q_TILE_A = 128
Q_TILE_A_DIRECT = 128
Q_TILES_PER_STEP_A = 2
Q_BLOCKS_PER_STEP_B = 8
VT_CHUNK = 128
SUM_ROWS = 16
DIRECT_TILES_PER_TRIP = 16
MAX_DIRECT_LOGIT = 32.0

FM_ROWS = Q_COLS + KV_COLS + KV_COLS + Q_COLS + KV_COLS
TM_COLS = KV_COLS + 2 * D_MODEL


def _resident(shape):
    nd = len(shape)
    return pl.BlockSpec(shape, lambda *_: (0,) * nd, pipeline_mode=pl.Buffered(1))


def _rms(x, g):
    ms = jnp.mean(x * x, axis=-1, keepdims=True)
    return x * lax.rsqrt(ms + EPS) * g


def _sigmoid(x):
    return 0.5 * jnp.tanh(0.5 * x) + 0.5


def _prenorm_operand(x, g):
    r = lax.rsqrt(jnp.mean(x * x, axis=-1, keepdims=True) + EPS)
    return (x * g).astype(BF16), r


def _swiglu_half_step(x, g_ref, w_in_ref, w_out_ref):
    xg, r = _prenorm_operand(x, g_ref[...])
    ab = jnp.dot(xg, w_in_ref[...], preferred_element_type=F32)
    ah = ab[:, :D_FF] * (0.5 * r)
    b = ab[:, D_FF:] * r
    act = (ah * (1.0 + jnp.tanh(ah)) * b).astype(BF16)
    return x + 0.5 * jnp.dot(act, w_out_ref[...], preferred_element_type=F32)


def _head_norm_rope_t(z, gain, cs):
    ms = jnp.mean(z * z, axis=0, keepdims=True)
    z = z * lax.rsqrt(ms + EPS) * gain
    cr, sr, cc, sc = cs[0:ROT], cs[ROT:2 * ROT], cs[2 * ROT:3 * ROT], cs[3 * ROT:4 * ROT]
    x1r, x2r = z[0:ROT], z[ROT:2 * ROT]
    x1c, x2c = z[2 * ROT:3 * ROT], z[3 * ROT:4 * ROT]
    return jnp.concatenate(
        [x1r * cr - x2r * sr, x2r * cr + x1r * sr, x1c * cc - x2c * sc, x2c * cc + x1c * sc], axis=0)


def _front_kernel(x_ref, n1_ref, w1i_ref, w1o_ref, nm_ref, wfm_ref, wtm_ref, qn_ref, kn_ref, cs_ref,
                  x1_ref, qa_ref, ka_ref, va_ref, qb_ref, kb_ref, vb_ref, g_ref):
    tm = x_ref.shape[0]
    x1 = _swiglu_half_step(x_ref[...], n1_ref, w1i_ref, w1o_ref)
    x1_ref[...] = x1
    h, r = _prenorm_operand(x1, nm_ref[...])
    r_t = jnp.broadcast_to(r, (tm, V7X_LANES)).T[:1]
    pt = lax.dot_general(wfm_ref[...], h, (((1,), (1,)), ((), ())), preferred_element_type=F32) * r_t
    ptm = jnp.dot(h, wtm_ref[...], preferred_element_type=F32) * r
    cs = cs_ref[...]
    qn = qn_ref[...]
    kn = kn_ref[...]
    scale = math.log2(math.e) / math.sqrt(HEAD_DIM)
    for hd in range(N_HEADS):
        z = pt[hd * HEAD_DIM:(hd + 1) * HEAD_DIM]
        qa_ref[0, hd * HEAD_DIM:(hd + 1) * HEAD_DIM, :] = (_head_norm_rope_t(z, qn, cs) * scale).astype(BF16)
    ka_t = jnp.concatenate(
        [_head_norm_rope_t(pt[Q_COLS + kv * HEAD_DIM:Q_COLS + (kv + 1) * HEAD_DIM], kn, cs) for kv in range(N_KV)],
        axis=0)
    ka_ref[0] = ka_t.T.astype(BF16)
    off = Q_COLS + KV_COLS
    va = pt[off:off + KV_COLS].astype(BF16)
    off += KV_COLS
    qb_ref[0] = (pt[off:off + Q_COLS] * scale).astype(BF16)
    off += Q_COLS
    vb = pt[off:off + KV_COLS].astype(BF16)
    for c in range(tm // VT_CHUNK):
        va_ref[0, c] = va[:, c * VT_CHUNK:(c + 1) * VT_CHUNK]
        vb_ref[0, c] = vb[:, c * VT_CHUNK:(c + 1) * VT_CHUNK]
    kb_ref[0] = ptm[:, :KV_COLS].astype(BF16)
    g_ref[...] = ptm[:, KV_COLS:].astype(BF16)


def _front(x, p, cs, batch, seq):
    tm = TOKEN_TILE
    n = batch * seq
    nt = seq // tm
    grid = (n // tm,)
    tok = lambda i: (i, 0)
    bt3 = lambda i: (i // nt, 0, i % nt)
    out_shape = (
        jax.ShapeDtypeStruct((n, D_MODEL), F32),
        jax.ShapeDtypeStruct((batch, Q_COLS, seq), BF16),
        jax.ShapeDtypeStruct((batch, seq, KV_COLS), BF16),
        jax.ShapeDtypeStruct((batch, seq // VT_CHUNK, KV_COLS, VT_CHUNK), BF16),
        jax.ShapeDtypeStruct((batch, Q_COLS, seq), BF16),
        jax.ShapeDtypeStruct((batch, seq, KV_COLS), BF16),
        jax.ShapeDtypeStruct((batch, seq // VT_CHUNK, KV_COLS, VT_CHUNK), BF16),
        jax.ShapeDtypeStruct((n, 2 * D_MODEL), BF16),
    )
    vt_spec = pl.BlockSpec((1, tm // VT_CHUNK, KV_COLS, VT_CHUNK), lambda i: (i // nt, i % nt, 0, 0))
    out_specs = (
        pl.BlockSpec((tm, D_MODEL), tok),
        pl.BlockSpec((1, Q_COLS, tm), bt3),
        pl.BlockSpec((1, tm, KV_COLS), lambda i: (i // nt, i % nt, 0)),
        vt_spec,
        pl.BlockSpec((1, Q_COLS, tm), bt3),
        pl.BlockSpec((1, tm, KV_COLS), lambda i: (i // nt, i % nt, 0)),
        vt_spec,
        pl.BlockSpec((tm, 2 * D_MODEL), tok),
    )
    in_specs = [
        pl.BlockSpec((tm, D_MODEL), tok),
        _resident((1, D_MODEL)),
        _resident((D_MODEL, 2 * D_FF)),
        _resident((D_FF, D_MODEL)),
        _resident((1, D_MODEL)),
        _resident((FM_ROWS, D_MODEL)),
        _resident((D_MODEL, TM_COLS)),
        _resident((HEAD_DIM, 1)),
        _resident((HEAD_DIM, 1)),
        pl.BlockSpec((4 * ROT, tm), lambda i: (0, i % nt)),
    ]
    return pl.pallas_call(
        _front_kernel,
        grid=grid,
        in_specs=in_specs,
        out_specs=out_specs,
        out_shape=out_shape,
        compiler_params=pltpu.CompilerParams(
            dimension_semantics=("arbitrary",), vmem_limit_bytes=TOKEN_KERNEL_VMEM_LIMIT),
        name="front",
    )(x, p["n1"], p["w1i"], p["w1o"], p["nm"], p["wfm"], p["wtm"], p["qn"], p["kn"], cs)


def _build_q_weights(q, wq_ref, tq):
    zeros = jnp.zeros((HEAD_DIM, tq), BF16)
    for hd in range(N_HEADS):
        qh = q[hd * HEAD_DIM:(hd + 1) * HEAD_DIM]
        blk = jnp.concatenate([qh, zeros], axis=0) if hd < GROUP else jnp.concatenate([zeros, qh], axis=0)
        wq_ref[:, hd * tq:(hd + 1) * tq] = blk


def _attn_a_online_kernel(q_ref, k_ref, v_ref, o_ref, wq_ref, s0_ref, s1_ref, t0_ref, t1_ref, m_ref, l_ref,
                          acc_ref, *, seq):
    tq = q_ref.shape[2]
    half = GROUP * tq
    tk = KEY_TILE_A
    cpt = tk // VT_CHUNK
    nk = seq // tk
    _build_q_weights(q_ref[0], wq_ref, tq)
    m_ref[...] = jnp.full(m_ref.shape, NEG_INF, F32)
    l_ref[...] = jnp.zeros(l_ref.shape, F32)
    acc_ref[...] = jnp.zeros(acc_ref.shape, F32)

    def scores(j, s_ref, t_ref):
        kt = k_ref[0, pl.ds(pl.multiple_of(j * tk, tk), tk), :]
        s = jnp.dot(kt, wq_ref[...], preferred_element_type=F32)
        s_ref[...] = s
        t_ref[...] = jnp.max(s, axis=0, keepdims=True)

    def softmax_pv(j, s_ref, t_ref):
        m_old = m_ref[...]
        m_new = jnp.maximum(m_old, t_ref[...])
        alpha = jnp.exp2(m_old - m_new)
        p = jnp.exp2(s_ref[...] - m_new)
        l_ref[...] = alpha * l_ref[...] + jnp.sum(p, axis=0, keepdims=True)
        m_ref[...] = m_new
        pb = p.astype(BF16)
        vt = jnp.concatenate([v_ref[0, j * cpt + c] for c in range(cpt)], axis=1)
        for kv in range(N_KV):
            cols = slice(kv * half, (kv + 1) * half)
            pv = jnp.dot(vt[kv * HEAD_DIM:(kv + 1) * HEAD_DIM], pb[:, cols], preferred_element_type=F32)
            acc_ref[kv] = acc_ref[kv] * alpha[:, cols] + pv

    scores(0, s0_ref, t0_ref)

    def body(jj, carry):
        j = 2 * jj
        scores(j + 1, s1_ref, t1_ref)
        softmax_pv(j, s0_ref, t0_ref)
        scores(jnp.minimum(j + 2, nk - 1), s0_ref, t0_ref)
        softmax_pv(j + 1, s1_ref, t1_ref)
        return carry

    lax.fori_loop(0, nk // 2, body, 0)
    inv_l = 1.0 / l_ref[...]
    rows = []
    for hd in range(N_HEADS):
        kv, g = divmod(hd, GROUP)
        cols = slice(g * tq, (g + 1) * tq)
        rows.append(acc_ref[kv][:, cols] * inv_l[:, hd * tq:(hd + 1) * tq])
    o_ref[0] = jnp.concatenate(rows, axis=0).T.astype(BF16)


def _attn_a_online(qt, k, vt, batch, seq):
    tq = Q_TILE_A
    assert (seq // KEY_TILE_A) % 2 == 0
    kernel = functools.partial(_attn_a_online_kernel, seq=seq)
    row = pltpu.VMEM((1, N_HEADS * tq), F32)
    tile = pltpu.VMEM((KEY_TILE_A, N_HEADS * tq), F32)
    return pl.pallas_call(
        kernel,
        grid=(batch, seq // tq),
        in_specs=[
            pl.BlockSpec((1, Q_COLS, tq), lambda b, i: (b, 0, i)),
            pl.BlockSpec((1, seq, KV_COLS), lambda b, i: (b, 0, 0)),
            pl.BlockSpec((1, seq // VT_CHUNK, KV_COLS, VT_CHUNK), lambda b, i: (b, 0, 0, 0)),
        ],
        out_specs=pl.BlockSpec((1, tq, Q_COLS), lambda b, i: (b, i, 0)),
        out_shape=jax.ShapeDtypeStruct((batch, seq, Q_COLS), BF16),
        scratch_shapes=[
            pltpu.VMEM((KV_COLS, N_HEADS * tq), BF16),
            tile, tile,
            row, row,
            row, row,
            pltpu.VMEM((N_KV, HEAD_DIM, GROUP * tq), F32),
        ],
        compiler_params=pltpu.CompilerParams(
            dimension_semantics=("arbitrary", "arbitrary"), vmem_limit_bytes=ATTN_VMEM_LIMIT),
        name="attn_a_online",
    )(qt, k, vt)


def _attn_a_direct_kernel(q_ref, k_ref, v_ref, o_ref, wq_ref, p0_ref, p1_ref, acc_ref, *, seq, per_trip):
    tq = Q_TILE_A_DIRECT
    half = GROUP * tq
    tk = KEY_TILE_A_DIRECT
    cpt = tk // VT_CHUNK
    nk = seq // tk
    p_refs = (p0_ref, p1_ref)

    def probs(sub, j, p_ref):
        kt = k_ref[0, pl.ds(pl.multiple_of(j * tk, tk), tk), :]
        s = jnp.dot(kt, wq_ref[sub], preferred_element_type=F32)
        p_ref[...] = jnp.exp2(s).astype(BF16)

    def pv(sub, j, p_ref):
        vt = jnp.concatenate([v_ref[0, j * cpt + c] for c in range(cpt)], axis=1)
        ones = jnp.ones((SUM_ROWS, tk), BF16)
        for kv in range(N_KV):
            cols = slice(kv * half, (kv + 1) * half)
            lhs = jnp.concatenate([vt[kv * HEAD_DIM:(kv + 1) * HEAD_DIM], ones], axis=0)
            acc_ref[sub, kv] += jnp.dot(lhs, p_ref[:, cols], preferred_element_type=F32)

    def start(sub):
        _build_q_weights(q_ref[0, :, sub * tq:(sub + 1) * tq], wq_ref.at[sub], tq)
        acc_ref[sub] = jnp.zeros(acc_ref.shape[1:], F32)
        probs(sub, 0, p0_ref)

    def finish(sub):
        rows = []
        for hd in range(N_HEADS):
            kv, g = divmod(hd, GROUP)
            cols = slice(g * tq, (g + 1) * tq)
            acc = acc_ref[sub, kv]
            rows.append(acc[:HEAD_DIM, cols] * (1.0 / acc[HEAD_DIM:HEAD_DIM + 1, cols]))
        o_ref[0, sub * tq:(sub + 1) * tq, :] = jnp.concatenate(rows, axis=0).T.astype(BF16)

    start(0)
    for sub in range(Q_TILES_PER_STEP_A):

        def body(jj, carry, sub=sub):
            for u in range(per_trip):
                probs(sub, per_trip * jj + u + 1, p_refs[(u + 1) % 2])
                pv(sub, per_trip * jj + u, p_refs[u % 2])
            return carry

        lax.fori_loop(0, nk // per_trip - 1, body, 0)
        j0 = nk - per_trip
        for u in range(per_trip):
            if u < per_trip - 1:
                probs(sub, j0 + u + 1, p_refs[(u + 1) % 2])
            elif sub + 1 < Q_TILES_PER_STEP_A:
                start(sub + 1)
            pv(sub, j0 + u, p_refs[u % 2])
        finish(sub)


def _attn_a_direct(qt, k, vt, batch, seq):
    tq = Q_TILE_A_DIRECT
    tqs = Q_TILES_PER_STEP_A * tq
    nk = seq // KEY_TILE_A_DIRECT
    per_trip = min(DIRECT_TILES_PER_TRIP, nk)
    assert per_trip % 2 == 0 and nk % per_trip == 0 and seq % tqs == 0
    kernel = functools.partial(_attn_a_direct_kernel, seq=seq, per_trip=per_trip)
    p_tile = pltpu.VMEM((KEY_TILE_A_DIRECT, N_HEADS * tq), BF16)
    return pl.pallas_call(
        kernel,
        grid=(batch, seq // tqs),
        in_specs=[
            pl.BlockSpec((1, Q_COLS, tqs), lambda b, i: (b, 0, i)),
            pl.BlockSpec((1, seq, KV_COLS), lambda b, i: (b, 0, 0)),
            pl.BlockSpec((1, seq // VT_CHUNK, KV_COLS, VT_CHUNK), lambda b, i: (b, 0, 0, 0)),
        ],
        out_specs=pl.BlockSpec((1, tqs, Q_COLS), lambda b, i: (b, i, 0)),
        out_shape=jax.ShapeDtypeStruct((batch, seq, Q_COLS), BF16),
        scratch_shapes=[
            pltpu.VMEM((Q_TILES_PER_STEP_A, KV_COLS, N_HEADS * tq), BF16),
            p_tile, p_tile,
            pltpu.VMEM((Q_TILES_PER_STEP_A, N_KV, HEAD_DIM + SUM_ROWS, GROUP * tq), F32),
        ],
        compiler_params=pltpu.CompilerParams(
            dimension_semantics=("arbitrary", "arbitrary"), vmem_limit_bytes=ATTN_VMEM_LIMIT),
        name="attn_a_direct",
    )(qt, k, vt)


def _attn_a(qt, k, vt, direct_ok, batch, seq):
    return lax.cond(direct_ok,
                    lambda: _attn_a_direct(qt, k, vt, batch, seq),
                    lambda: _attn_a_online(qt, k, vt, batch, seq))


def _attn_b_kernel(rb_ref, q_ref, k_ref, v_ref, bucket_ref, sink_ref, o_ref, wq_ref, bias_ref, *, seq):
    step = pl.program_id(1)

    @pl.when((pl.program_id(0) == 0) & (step == 0))
    def _():
        _window_bias_tables(rb_ref, bucket_ref, bias_ref)

    sink = sink_ref[...] * math.log2(math.e)
    logits = []
    for sub in range(Q_BLOCKS_PER_STEP_B):
        wq = wq_ref.at[sub]
        _build_q_weights(q_ref[0, :, sub * Q_BLOCK:(sub + 1) * Q_BLOCK], wq, Q_BLOCK)
        logits.append(_window_logits(step * Q_BLOCKS_PER_STEP_B + sub, wq, k_ref, bias_ref, seq))
    for sub, (s, idxs) in enumerate(logits):
        o_ref[0, sub * Q_BLOCK:(sub + 1) * Q_BLOCK, :] = _window_softmax_pv(s, idxs, sink, v_ref)


def _window_bias_tables(rb_ref, bucket_ref, bias_ref):
    tq = Q_BLOCK
    log2e = math.log2(math.e)
    for jb in range(3):
        bucket = bucket_ref[jb]
        key = lax.broadcasted_iota(jnp.int32, (tq, tq), 0)
        qry = lax.broadcasted_iota(jnp.int32, (tq, tq), 1)
        rel = (jb - 1) * tq + key - qry
        inside = jnp.abs(rel) <= WINDOW
        for hd in range(N_HEADS):
            tbl = jnp.zeros((tq, tq), F32)
            for b in range(N_BUCKETS):
                tbl = jnp.where(bucket == b, rb_ref[b, hd], tbl)
            bias_ref[jb, :, hd * tq:(hd + 1) * tq] = jnp.where(inside, tbl * log2e, NEG_INF)
    bias_ref[3] = jnp.full(bias_ref.shape[1:], NEG_INF, F32)


def _window_logits(i, wq, k_ref, bias_ref, seq):
    tq = Q_BLOCK
    nb = seq // tq
    parts = []
    idxs = []
    for jb in range(3):
        blk = i - 1 + jb
        valid = (blk >= 0) & (blk < nb)
        idx = jnp.clip(blk, 0, nb - 1)
        kt = k_ref[0, pl.ds(pl.multiple_of(idx * tq, tq), tq), :]
        tbl = bias_ref[jnp.where(valid, jb, 3)]
        parts.append(jnp.dot(kt, wq[...], preferred_element_type=F32) + tbl)
        idxs.append(idx)
    return jnp.concatenate(parts, axis=0), idxs


def _window_softmax_pv(s, idxs, sink, v_ref):
    tq = Q_BLOCK
    half = GROUP * tq
    ones = jnp.ones((SUM_ROWS, tq), BF16)
    m = jnp.maximum(jnp.max(s, axis=0, keepdims=True), sink)
    pb = jnp.exp2(s - m).astype(BF16)
    p_sink = jnp.exp2(sink - m)
    rows = []
    for kv in range(N_KV):
        cols = slice(kv * half, (kv + 1) * half)
        acc = jnp.zeros((HEAD_DIM + SUM_ROWS, half), F32)
        for jb in range(3):
            lhs = jnp.concatenate([v_ref[0, idxs[jb]][kv * HEAD_DIM:(kv + 1) * HEAD_DIM], ones], axis=0)
            acc = acc + jnp.dot(lhs, pb[jb * tq:(jb + 1) * tq, cols], preferred_element_type=F32)
        out = acc[:HEAD_DIM] * (1.0 / (acc[HEAD_DIM:HEAD_DIM + 1] + p_sink[:, cols]))
        for g in range(GROUP):
            rows.append(out[:, g * tq:(g + 1) * tq])
    return jnp.concatenate(rows, axis=0).T.astype(BF16)


def _attn_b(qt, k, vt, rel_bias, bucket_t, sink_row, batch, seq):
    tq = Q_BLOCK
    tqs = Q_BLOCKS_PER_STEP_B * tq
    assert seq % tqs == 0
    kernel = functools.partial(_attn_b_kernel, seq=seq)
    return pl.pallas_call(
        kernel,
        grid=(batch, seq // tqs),
        in_specs=[
            pl.BlockSpec(memory_space=pltpu.SMEM),
            pl.BlockSpec((1, Q_COLS, tqs), lambda b, i: (b, 0, i)),
            pl.BlockSpec((1, seq, KV_COLS), lambda b, i: (b, 0, 0)),
            pl.BlockSpec((1, seq // VT_CHUNK, KV_COLS, VT_CHUNK), lambda b, i: (b, 0, 0, 0)),
            pl.BlockSpec((3, tq, tq), lambda b, i: (0, 0, 0)),
            pl.BlockSpec((1, N_HEADS * tq), lambda b, i: (0, 0)),
        ],
        out_specs=pl.BlockSpec((1, tqs, Q_COLS), lambda b, i: (b, i, 0)),
        out_shape=jax.ShapeDtypeStruct((batch, seq, Q_COLS), BF16),
        scratch_shapes=[
            pltpu.VMEM((Q_BLOCKS_PER_STEP_B, KV_COLS, N_HEADS * tq), BF16),
            pltpu.VMEM((4, tq, N_HEADS * tq), F32),
        ],
        compiler_params=pltpu.CompilerParams(
            dimension_semantics=("arbitrary", "arbitrary"), vmem_limit_bytes=ATTN_VMEM_LIMIT),
        name="attn_b",
    )(rel_bias, qt, k, vt, bucket_t, sink_row)


def _back_kernel(x1_ref, oa_ref, ob_ref, g_ref, wba_ref, wbb_ref, wo_ref, n2_ref, w2i_ref, w2o_ref, nf_ref, y_ref):
    ya = jnp.dot(oa_ref[...], wba_ref[...], preferred_element_type=F32)
    yb = jnp.dot(ob_ref[...], wbb_ref[...], preferred_element_type=F32)
    g = g_ref[...].astype(F32)
    merged = _sigmoid(g[:, :D_MODEL]) * ya + _sigmoid(g[:, D_MODEL:]) * yb
    x2 = x1_ref[...] + jnp.dot(merged.astype(BF16), wo_ref[...], preferred_element_type=F32)
    x3 = _swiglu_half_step(x2, n2_ref, w2i_ref, w2o_ref)
    y_ref[...] = _rms(x3, nf_ref[...])


def _back(x1, oa, ob, g, p):
    tm = TOKEN_TILE
    n = x1.shape[0]
    tok = lambda i: (i, 0)
    return pl.pallas_call(
        _back_kernel,
        grid=(n // tm,),
        in_specs=[
            pl.BlockSpec((tm, D_MODEL), tok),
            pl.BlockSpec((tm, Q_COLS), tok),
            pl.BlockSpec((tm, Q_COLS), tok),
            pl.BlockSpec((tm, 2 * D_MODEL), tok),
            _resident((Q_COLS, D_MODEL)),
            _resident((Q_COLS, D_MODEL)),
            _resident((D_MODEL, D_MODEL)),
            _resident((1, D_MODEL)),
            _resident((D_MODEL, 2 * D_FF)),
            _resident((D_FF, D_MODEL)),
            _resident((1, D_MODEL)),
        ],
        out_specs=pl.BlockSpec((tm, D_MODEL), tok),
        out_shape=jax.ShapeDtypeStruct((n, D_MODEL), F32),
        compiler_params=pltpu.CompilerParams(
            dimension_semantics=("arbitrary",), vmem_limit_bytes=TOKEN_KERNEL_VMEM_LIMIT),
        name="back",
    )(x1, oa, ob, g, p["wba"], p["wbb"], p["wo"], p["n2"], p["w2i"], p["w2o"], p["nf"])


def _t5_bucket(rel):
    nb = N_BUCKETS // 2
    max_exact = nb // 2
    ret = jnp.where(rel > 0, nb, 0)
    n = jnp.abs(rel)
    large = max_exact + (jnp.log(jnp.maximum(n, 1).astype(F32) / max_exact)
                         / math.log(MAX_DISTANCE / max_exact) * (nb - max_exact)).astype(jnp.int32)
    large = jnp.minimum(large, nb - 1)
    return ret + jnp.where(n < max_exact, n, large)


def _rope_table(seq):
    rows = seq // GRID_W
    grid_r, grid_c = jnp.meshgrid(jnp.arange(rows, dtype=F32), jnp.arange(GRID_W, dtype=F32), indexing="ij")
    pos = jnp.stack([grid_r.reshape(-1), grid_c.reshape(-1)], axis=-1)
    inv = ROPE_THETA ** (-jnp.arange(0, AXIS_DIM, 2, dtype=F32) / AXIS_DIM)
    ang = pos[:, :, None] * inv
    cos = jnp.cos(ang)
    sin = jnp.sin(ang)
    return jnp.concatenate([cos[:, 0].T, sin[:, 0].T, cos[:, 1].T, sin[:, 1].T], axis=0)


def _trunk(x, p, bucket_t, sink_row, rel_bias, direct_ok):
    batch, seq, _ = x.shape
    cs = _rope_table(seq)
    x1, qa, ka, va, qb, kb, vb, g = _front(x.reshape(batch * seq, D_MODEL), p, cs, batch, seq)
    oa = _attn_a(qa, ka, va, direct_ok, batch, seq)
    ob = _attn_b(qb, kb, vb, rel_bias, bucket_t, sink_row, batch, seq)
    y = _back(x1, oa.reshape(batch * seq, Q_COLS), ob.reshape(batch * seq, Q_COLS), g, p)
    return y.reshape(batch, seq, D_MODEL)


def kernel(x_prompt, x_sample, norm_ffn1, w_ffn1_in, w_ffn1_out, norm_mix, w_in, q_norm_a, k_norm_a, sink_b,
           w_branch_a, w_branch_b, w_out, norm_ffn2, w_ffn2_in, w_ffn2_out, rel_bias, norm_final):
    assert norm_ffn1.shape[0] == 1, "single-layer trunk"
    wi = w_in[0]
    c = 0
    parts = {}
    for name, width in (("qa", Q_COLS), ("ka", KV_COLS), ("va", KV_COLS), ("qb", Q_COLS), ("kb", KV_COLS),
                        ("vb", KV_COLS), ("ga", D_MODEL), ("gb", D_MODEL)):
        parts[name] = wi[:, c:c + width]
        c += width
    wfm = jnp.concatenate([parts[k] for k in ("qa", "ka", "va", "qb", "vb")], axis=1).T.astype(BF16)
    wtm = jnp.concatenate([parts[k] for k in ("kb", "ga", "gb")], axis=1).astype(BF16)
    p = dict(
        n1=norm_ffn1[0].reshape(1, D_MODEL), w1i=w_ffn1_in[0].astype(BF16), w1o=w_ffn1_out[0].astype(BF16),
        nm=norm_mix[0].reshape(1, D_MODEL), wfm=wfm, wtm=wtm,
        qn=q_norm_a[0].reshape(HEAD_DIM, 1), kn=k_norm_a[0].reshape(HEAD_DIM, 1),
        wba=w_branch_a[0].astype(BF16), wbb=w_branch_b[0].astype(BF16), wo=w_out[0].astype(BF16),
        n2=norm_ffn2[0].reshape(1, D_MODEL), w2i=w_ffn2_in[0].astype(BF16), w2o=w_ffn2_out[0].astype(BF16),
        nf=norm_final.reshape(1, D_MODEL),
    )
    key = jnp.arange(3 * Q_BLOCK)[:, None] - Q_BLOCK
    qry = jnp.arange(Q_BLOCK)[None, :]
    bucket_t = _t5_bucket(key - qry).astype(jnp.int32).reshape(3, Q_BLOCK, Q_BLOCK)
    sink_row = jnp.repeat(sink_b[0].astype(F32), Q_BLOCK).reshape(1, N_HEADS * Q_BLOCK)
    rb = rel_bias.astype(F32)
    logit_bound = (1.02 * HEAD_DIM * math.log2(math.e) / math.sqrt(HEAD_DIM)
                   * jnp.max(jnp.abs(q_norm_a[0])) * jnp.max(jnp.abs(k_norm_a[0])))
    direct_ok = logit_bound <= MAX_DIRECT_LOGIT
    y_prompt = _trunk(x_prompt, p, bucket_t, sink_row, rb, direct_ok)
    y_sample = _trunk(x_sample, p, bucket_t, sink_row, rb, direct_ok)
    return (y_prompt, y_sample)
```

```python
import functools
import math

import jax
import jax.numpy as jnp
from jax import lax
from jax.experimental import pallas as pl
from jax.experimental.pallas import tpu as pltpu

D_MODEL = 1024
GRID_W = 64
HEAD_DIM = 64
N_HEADS = 8
N_KV = 2
GROUP = N_HEADS // N_KV
Q_BLOCK = 128
WINDOW = 128
N_BUCKETS = 32
MAX_DISTANCE = 128
ROPE_THETA = 10000.0
AXIS_DIM = HEAD_DIM // 2
ROT = AXIS_DIM // 2
D_FF = 2816
EPS = 1e-6
NEG_INF = -1e30
Q_COLS = N_HEADS * HEAD_DIM
KV_COLS = N_KV * HEAD_DIM

F32 = jnp.float32
BF16 = jnp.bfloat16

V7X_LANES = 128
V7X_VMEM_BYTES = 64 * 1024 * 1024
MIB = 1024 * 1024
TOKEN_KERNEL_VMEM_LIMIT = V7X_VMEM_BYTES - 8 * MIB
ATTN_VMEM_LIMIT = 40 * MIB

TOKEN_TILE = 512
KEY_TILE_A = 256
KEY_TILE_A_DIRECT = 256
Q_TILE_A = 128
Q_TILE_A_DIRECT = 128
Q_TILES_PER_STEP_A = 2
Q_BLOCKS_PER_STEP_B = 8
VT_CHUNK = 128
SUM_ROWS = 16
DIRECT_TILES_PER_TRIP = 16
MAX_DIRECT_LOGIT = 32.0

FM_ROWS = Q_COLS + KV_COLS + KV_COLS + Q_COLS + KV_COLS
TM_COLS = KV_COLS + 2 * D_MODEL


def _resident(shape):
    nd = len(shape)
    return pl.BlockSpec(shape, lambda *_: (0,) * nd, pipeline_mode=pl.Buffered(1))


def _rms(x, g):
    ms = jnp.mean(x * x, axis=-1, keepdims=True)
    return x * lax.rsqrt(ms + EPS) * g


def _sigmoid(x):
    return 0.5 * jnp.tanh(0.5 * x) + 0.5


def _prenorm_operand(x, g):
    r = lax.rsqrt(jnp.mean(x * x, axis=-1, keepdims=True) + EPS)
    return (x * g).astype(BF16), r


def _swiglu_half_step(x, g_ref, w_in_ref, w_out_ref):
    xg, r = _prenorm_operand(x, g_ref[...])
    ab = jnp.dot(xg, w_in_ref[...], preferred_element_type=F32)
    ah = ab[:, :D_FF] * (0.5 * r)
    b = ab[:, D_FF:] * r
    act = (ah * (1.0 + jnp.tanh(ah)) * b).astype(BF16)
    return x + 0.5 * jnp.dot(act, w_out_ref[...], preferred_element_type=F32)


def _head_norm_rope_t(z, gain, cs):
    ms = jnp.mean(z * z, axis=0, keepdims=True)
    z = z * lax.rsqrt(ms + EPS) * gain
    cr, sr, cc, sc = cs[0:ROT], cs[ROT:2 * ROT], cs[2 * ROT:3 * ROT], cs[3 * ROT:4 * ROT]
    x1r, x2r = z[0:ROT], z[ROT:2 * ROT]
    x1c, x2c = z[2 * ROT:3 * ROT], z[3 * ROT:4 * ROT]
    return jnp.concatenate(
        [x1r * cr - x2r * sr, x2r * cr + x1r * sr, x1c * cc - x2c * sc, x2c * cc + x1c * sc], axis=0)


def _front_kernel(x_ref, n1_ref, w1i_ref, w1o_ref, nm_ref, wfm_ref, wtm_ref, qn_ref, kn_ref, cs_ref,
                  x1_ref, qa_ref, ka_ref, va_ref, qb_ref, kb_ref, vb_ref, g_ref):
    tm = x_ref.shape[0]
    x1 = _swiglu_half_step(x_ref[...], n1_ref, w1i_ref, w1o_ref)
    x1_ref[...] = x1
    h, r = _prenorm_operand(x1, nm_ref[...])
    r_t = jnp.broadcast_to(r, (tm, V7X_LANES)).T[:1]
    pt = lax.dot_general(wfm_ref[...], h, (((1,), (1,)), ((), ())), preferred_element_type=F32) * r_t
    ptm = jnp.dot(h, wtm_ref[...], preferred_element_type=F32) * r
    cs = cs_ref[...]
    qn = qn_ref[...]
    kn = kn_ref[...]
    scale = math.log2(math.e) / math.sqrt(HEAD_DIM)
    for hd in range(N_HEADS):
        z = pt[hd * HEAD_DIM:(hd + 1) * HEAD_DIM]
        qa_ref[0, hd * HEAD_DIM:(hd + 1) * HEAD_DIM, :] = (_head_norm_rope_t(z, qn, cs) * scale).astype(BF16)
    ka_t = jnp.concatenate(
        [_head_norm_rope_t(pt[Q_COLS + kv * HEAD_DIM:Q_COLS + (kv + 1) * HEAD_DIM], kn, cs) for kv in range(N_KV)],
        axis=0)
    ka_ref[0] = ka_t.T.astype(BF16)
    off = Q_COLS + KV_COLS
    va = pt[off:off + KV_COLS].astype(BF16)
    off += KV_COLS
    qb_ref[0] = (pt[off:off + Q_COLS] * scale).astype(BF16)
    off += Q_COLS
    vb = pt[off:off + KV_COLS].astype(BF16)
    for c in range(tm // VT_CHUNK):
        va_ref[0, c] = va[:, c * VT_CHUNK:(c + 1) * VT_CHUNK]
        vb_ref[0, c] = vb[:, c * VT_CHUNK:(c + 1) * VT_CHUNK]
    kb_ref[0] = ptm[:, :KV_COLS].astype(BF16)
    g_ref[...] = ptm[:, KV_COLS:].astype(BF16)


def _front(x, p, cs, batch, seq):
    tm = TOKEN_TILE
    n = batch * seq
    nt = seq // tm
    grid = (n // tm,)
    tok = lambda i: (i, 0)
    bt3 = lambda i: (i // nt, 0, i % nt)
    out_shape = (
        jax.ShapeDtypeStruct((n, D_MODEL), F32),
        jax.ShapeDtypeStruct((batch, Q_COLS, seq), BF16),
        jax.ShapeDtypeStruct((batch, seq, KV_COLS), BF16),
        jax.ShapeDtypeStruct((batch, seq // VT_CHUNK, KV_COLS, VT_CHUNK), BF16),
        jax.ShapeDtypeStruct((batch, Q_COLS, seq), BF16),
        jax.ShapeDtypeStruct((batch, seq, KV_COLS), BF16),
        jax.ShapeDtypeStruct((batch, seq // VT_CHUNK, KV_COLS, VT_CHUNK), BF16),
        jax.ShapeDtypeStruct((n, 2 * D_MODEL), BF16),
    )
    vt_spec = pl.BlockSpec((1, tm // VT_CHUNK, KV_COLS, VT_CHUNK), lambda i: (i // nt, i % nt, 0, 0))
    out_specs = (
        pl.BlockSpec((tm, D_MODEL), tok),
        pl.BlockSpec((1, Q_COLS, tm), bt3),
        pl.BlockSpec((1, tm, KV_COLS), lambda i: (i // nt, i % nt, 0)),
        vt_spec,
        pl.BlockSpec((1, Q_COLS, tm), bt3),
        pl.BlockSpec((1, tm, KV_COLS), lambda i: (i // nt, i % nt, 0)),
        vt_spec,
        pl.BlockSpec((tm, 2 * D_MODEL), tok),
    )
    in_specs = [
        pl.BlockSpec((tm, D_MODEL), tok),
        _resident((1, D_MODEL)),
        _resident((D_MODEL, 2 * D_FF)),
        _resident((D_FF, D_MODEL)),
        _resident((1, D_MODEL)),
        _resident((FM_ROWS, D_MODEL)),
        _resident((D_MODEL, TM_COLS)),
        _resident((HEAD_DIM, 1)),
        _resident((HEAD_DIM, 1)),
        pl.BlockSpec((4 * ROT, tm), lambda i: (0, i % nt)),
    ]
    return pl.pallas_call(
        _front_kernel,
        grid=grid,
        in_specs=in_specs,
        out_specs=out_specs,
        out_shape=out_shape,
        compiler_params=pltpu.CompilerParams(
            dimension_semantics=("arbitrary",), vmem_limit_bytes=TOKEN_KERNEL_VMEM_LIMIT),
        name="front",
    )(x, p["n1"], p["w1i"], p["w1o"], p["nm"], p["wfm"], p["wtm"], p["qn"], p["kn"], cs)


def _build_q_weights(q, wq_ref, tq):
    zeros = jnp.zeros((HEAD_DIM, tq), BF16)
    for hd in range(N_HEADS):
        qh = q[hd * HEAD_DIM:(hd + 1) * HEAD_DIM]
        blk = jnp.concatenate([qh, zeros], axis=0) if hd < GROUP else jnp.concatenate([zeros, qh], axis=0)
        wq_ref[:, hd * tq:(hd + 1) * tq] = blk


def _attn_a_online_kernel(q_ref, k_ref, v_ref, o_ref, wq_ref, s0_ref, s1_ref, t0_ref, t1_ref, m_ref, l_ref,
                          acc_ref, *, seq):
    tq = q_ref.shape[2]
    half = GROUP * tq
    tk = KEY_TILE_A
    cpt = tk // VT_CHUNK
    nk = seq // tk
    _build_q_weights(q_ref[0], wq_ref, tq)
    m_ref[...] = jnp.full(m_ref.shape, NEG_INF, F32)
    l_ref[...] = jnp.zeros(l_ref.shape, F32)
    acc_ref[...] = jnp.zeros(acc_ref.shape, F32)

    def scores(j, s_ref, t_ref):
        kt = k_ref[0, pl.ds(pl.multiple_of(j * tk, tk), tk), :]
        s = jnp.dot(kt, wq_ref[...], preferred_element_type=F32)
        s_ref[...] = s
        t_ref[...] = jnp.max(s, axis=0, keepdims=True)

    def softmax_pv(j, s_ref, t_ref):
        m_old = m_ref[...]
        m_new = jnp.maximum(m_old, t_ref[...])
        alpha = jnp.exp2(m_old - m_new)
        p = jnp.exp2(s_ref[...] - m_new)
        l_ref[...] = alpha * l_ref[...] + jnp.sum(p, axis=0, keepdims=True)
        m_ref[...] = m_new
        pb = p.astype(BF16)
        vt = jnp.concatenate([v_ref[0, j * cpt + c] for c in range(cpt)], axis=1)
        for kv in range(N_KV):
            cols = slice(kv * half, (kv + 1) * half)
            pv = jnp.dot(vt[kv * HEAD_DIM:(kv + 1) * HEAD_DIM], pb[:, cols], preferred_element_type=F32)
            acc_ref[kv] = acc_ref[kv] * alpha[:, cols] + pv

    scores(0, s0_ref, t0_ref)

    def body(jj, carry):
        j = 2 * jj
        scores(j + 1, s1_ref, t1_ref)
        softmax_pv(j, s0_ref, t0_ref)
        scores(jnp.minimum(j + 2, nk - 1), s0_ref, t0_ref)
        softmax_pv(j + 1, s1_ref, t1_ref)
        return carry

    lax.fori_loop(0, nk // 2, body, 0)
    inv_l = 1.0 / l_ref[...]
    rows = []
    for hd in range(N_HEADS):
        kv, g = divmod(hd, GROUP)
        cols = slice(g * tq, (g + 1) * tq)
        rows.append(acc_ref[kv][:, cols] * inv_l[:, hd * tq:(hd + 1) * tq])
    o_ref[0] = jnp.concatenate(rows, axis=0).T.astype(BF16)


def _attn_a_online(qt, k, vt, batch, seq):
    tq = Q_TILE_A
    assert (seq // KEY_TILE_A) % 2 == 0
    kernel = functools.partial(_attn_a_online_kernel, seq=seq)
    row = pltpu.VMEM((1, N_HEADS * tq), F32)
    tile = pltpu.VMEM((KEY_TILE_A, N_HEADS * tq), F32)
    return pl.pallas_call(
        kernel,
        grid=(batch, seq // tq),
        in_specs=[
            pl.BlockSpec((1, Q_COLS, tq), lambda b, i: (b, 0, i)),
            pl.BlockSpec((1, seq, KV_COLS), lambda b, i: (b, 0, 0)),
            pl.BlockSpec((1, seq // VT_CHUNK, KV_COLS, VT_CHUNK), lambda b, i: (b, 0, 0, 0)),
        ],
        out_specs=pl.BlockSpec((1, tq, Q_COLS), lambda b, i: (b, i, 0)),
        out_shape=jax.ShapeDtypeStruct((batch, seq, Q_COLS), BF16),
        scratch_shapes=[
            pltpu.VMEM((KV_COLS, N_HEADS * tq), BF16),
            tile, tile,
            row, row,
            row, row,
            pltpu.VMEM((N_KV, HEAD_DIM, GROUP * tq), F32),
        ],
        compiler_params=pltpu.CompilerParams(
            dimension_semantics=("arbitrary", "arbitrary"), vmem_limit_bytes=ATTN_VMEM_LIMIT),
        name="attn_a_online",
    )(qt, k, vt)


def _attn_a_direct_kernel(q_ref, k_ref, v_ref, o_ref, wq_ref, p0_ref, p1_ref, acc_ref, *, seq, per_trip):
    tq = Q_TILE_A_DIRECT
    half = GROUP * tq
    tk = KEY_TILE_A_DIRECT
    cpt = tk // VT_CHUNK
    nk = seq // tk
    p_refs = (p0_ref, p1_ref)

    def probs(sub, j, p_ref):
        kt = k_ref[0, pl.ds(pl.multiple_of(j * tk, tk), tk), :]
        s = jnp.dot(kt, wq_ref[sub], preferred_element_type=F32)
        p_ref[...] = jnp.exp2(s).astype(BF16)

    def pv(sub, j, p_ref):
        vt = jnp.concatenate([v_ref[0, j * cpt + c] for c in range(cpt)], axis=1)
        ones = jnp.ones((SUM_ROWS, tk), BF16)
        for kv in range(N_KV):
            cols = slice(kv * half, (kv + 1) * half)
            lhs = jnp.concatenate([vt[kv * HEAD_DIM:(kv + 1) * HEAD_DIM], ones], axis=0)
            acc_ref[sub, kv] += jnp.dot(lhs, p_ref[:, cols], preferred_element_type=F32)

    def start(sub):
        _build_q_weights(q_ref[0, :, sub * tq:(sub + 1) * tq], wq_ref.at[sub], tq)
        acc_ref[sub] = jnp.zeros(acc_ref.shape[1:], F32)
        probs(sub, 0, p0_ref)

    def finish(sub):
        rows = []
        for hd in range(N_HEADS):
            kv, g = divmod(hd, GROUP)
            cols = slice(g * tq, (g + 1) * tq)
            acc = acc_ref[sub, kv]
            rows.append(acc[:HEAD_DIM, cols] * (1.0 / acc[HEAD_DIM:HEAD_DIM + 1, cols]))
        o_ref[0, sub * tq:(sub + 1) * tq, :] = jnp.concatenate(rows, axis=0).T.astype(BF16)

    start(0)
    for sub in range(Q_TILES_PER_STEP_A):

        def body(jj, carry, sub=sub):
            for u in range(per_trip):
                probs(sub, per_trip * jj + u + 1, p_refs[(u + 1) % 2])
                pv(sub, per_trip * jj + u, p_refs[u % 2])
            return carry

        lax.fori_loop(0, nk // per_trip - 1, body, 0)
        j0 = nk - per_trip
        for u in range(per_trip):
            if u < per_trip - 1:
                probs(sub, j0 + u + 1, p_refs[(u + 1) % 2])
            elif sub + 1 < Q_TILES_PER_STEP_A:
                start(sub + 1)
            pv(sub, j0 + u, p_refs[u % 2])
        finish(sub)


def _attn_a_direct(qt, k, vt, batch, seq):
    tq = Q_TILE_A_DIRECT
    tqs = Q_TILES_PER_STEP_A * tq
    nk = seq // KEY_TILE_A_DIRECT
    per_trip = min(DIRECT_TILES_PER_TRIP, nk)
    assert per_trip % 2 == 0 and nk % per_trip == 0 and seq % tqs == 0
    kernel = functools.partial(_attn_a_direct_kernel, seq=seq, per_trip=per_trip)
    p_tile = pltpu.VMEM((KEY_TILE_A_DIRECT, N_HEADS * tq), BF16)
    return pl.pallas_call(
        kernel,
        grid=(batch, seq // tqs),
        in_specs=[
            pl.BlockSpec((1, Q_COLS, tqs), lambda b, i: (b, 0, i)),
            pl.BlockSpec((1, seq, KV_COLS), lambda b, i: (b, 0, 0)),
            pl.BlockSpec((1, seq // VT_CHUNK, KV_COLS, VT_CHUNK), lambda b, i: (b, 0, 0, 0)),
        ],
        out_specs=pl.BlockSpec((1, tqs, Q_COLS), lambda b, i: (b, i, 0)),
        out_shape=jax.ShapeDtypeStruct((batch, seq, Q_COLS), BF16),
        scratch_shapes=[
            pltpu.VMEM((Q_TILES_PER_STEP_A, KV_COLS, N_HEADS * tq), BF16),
            p_tile, p_tile,
            pltpu.VMEM((Q_TILES_PER_STEP_A, N_KV, HEAD_DIM + SUM_ROWS, GROUP * tq), F32),
        ],
        compiler_params=pltpu.CompilerParams(
            dimension_semantics=("arbitrary", "arbitrary"), vmem_limit_bytes=ATTN_VMEM_LIMIT),
        name="attn_a_direct",
    )(qt, k, vt)


def _attn_a(qt, k, vt, direct_ok, batch, seq):
    return lax.cond(direct_ok,
                    lambda: _attn_a_direct(qt, k, vt, batch, seq),
                    lambda: _attn_a_online(qt, k, vt, batch, seq))


def _attn_b_kernel(rb_ref, q_ref, k_ref, v_ref, bucket_ref, sink_ref, o_ref, wq_ref, bias_ref, *, seq):
    step = pl.program_id(1)

    @pl.when((pl.program_id(0) == 0) & (step == 0))
    def _():
        _window_bias_tables(rb_ref, bucket_ref, bias_ref)

    sink = sink_ref[...] * math.log2(math.e)
    logits = []
    for sub in range(Q_BLOCKS_PER_STEP_B):
        wq = wq_ref.at[sub]
        _build_q_weights(q_ref[0, :, sub * Q_BLOCK:(sub + 1) * Q_BLOCK], wq, Q_BLOCK)
        logits.append(_window_logits(step * Q_BLOCKS_PER_STEP_B + sub, wq, k_ref, bias_ref, seq))
    for sub, (s, idxs) in enumerate(logits):
        o_ref[0, sub * Q_BLOCK:(sub + 1) * Q_BLOCK, :] = _window_softmax_pv(s, idxs, sink, v_ref)


def _window_bias_tables(rb_ref, bucket_ref, bias_ref):
    tq = Q_BLOCK
    log2e = math.log2(math.e)
    for jb in range(3):
        bucket = bucket_ref[jb]
        key = lax.broadcasted_iota(jnp.int32, (tq, tq), 0)
        qry = lax.broadcasted_iota(jnp.int32, (tq, tq), 1)
        rel = (jb - 1) * tq + key - qry
        inside = jnp.abs(rel) <= WINDOW
        for hd in range(N_HEADS):
            tbl = jnp.zeros((tq, tq), F32)
            for b in range(N_BUCKETS):
                tbl = jnp.where(bucket == b, rb_ref[b, hd], tbl)
            bias_ref[jb, :, hd * tq:(hd + 1) * tq] = jnp.where(inside, tbl * log2e, NEG_INF)
    bias_ref[3] = jnp.full(bias_ref.shape[1:], NEG_INF, F32)


def _window_logits(i, wq, k_ref, bias_ref, seq):
    tq = Q_BLOCK
    nb = seq // tq
    parts = []
    idxs = []
    for jb in range(3):
        blk = i - 1 + jb
        valid = (blk >= 0) & (blk < nb)
        idx = jnp.clip(blk, 0, nb - 1)
        kt = k_ref[0, pl.ds(pl.multiple_of(idx * tq, tq), tq), :]
        tbl = bias_ref[jnp.where(valid, jb, 3)]
        parts.append(jnp.dot(kt, wq[...], preferred_element_type=F32) + tbl)
        idxs.append(idx)
    return jnp.concatenate(parts, axis=0), idxs


def _window_softmax_pv(s, idxs, sink, v_ref):
    tq = Q_BLOCK
    half = GROUP * tq
    ones = jnp.ones((SUM_ROWS, tq), BF16)
    m = jnp.maximum(jnp.max(s, axis=0, keepdims=True), sink)
    pb = jnp.exp2(s - m).astype(BF16)
    p_sink = jnp.exp2(sink - m)
    rows = []
    for kv in range(N_KV):
        cols = slice(kv * half, (kv + 1) * half)
        acc = jnp.zeros((HEAD_DIM + SUM_ROWS, half), F32)
        for jb in range(3):
            lhs = jnp.concatenate([v_ref[0, idxs[jb]][kv * HEAD_DIM:(kv + 1) * HEAD_DIM], ones], axis=0)
            acc = acc + jnp.dot(lhs, pb[jb * tq:(jb + 1) * tq, cols], preferred_element_type=F32)
        out = acc[:HEAD_DIM] * (1.0 / (acc[HEAD_DIM:HEAD_DIM + 1] + p_sink[:, cols]))
        for g in range(GROUP):
            rows.append(out[:, g * tq:(g + 1) * tq])
    return jnp.concatenate(rows, axis=0).T.astype(BF16)


def _attn_b(qt, k, vt, rel_bias, bucket_t, sink_row, batch, seq):
    tq = Q_BLOCK
    tqs = Q_BLOCKS_PER_STEP_B * tq
    assert seq % tqs == 0
    kernel = functools.partial(_attn_b_kernel, seq=seq)
    return pl.pallas_call(
        kernel,
        grid=(batch, seq // tqs),
        in_specs=[
            pl.BlockSpec(memory_space=pltpu.SMEM),
            pl.BlockSpec((1, Q_COLS, tqs), lambda b, i: (b, 0, i)),
            pl.BlockSpec((1, seq, KV_COLS), lambda b, i: (b, 0, 0)),
            pl.BlockSpec((1, seq // VT_CHUNK, KV_COLS, VT_CHUNK), lambda b, i: (b, 0, 0, 0)),
            pl.BlockSpec((3, tq, tq), lambda b, i: (0, 0, 0)),
            pl.BlockSpec((1, N_HEADS * tq), lambda b, i: (0, 0)),
        ],
        out_specs=pl.BlockSpec((1, tqs, Q_COLS), lambda b, i: (b, i, 0)),
        out_shape=jax.ShapeDtypeStruct((batch, seq, Q_COLS), BF16),
        scratch_shapes=[
            pltpu.VMEM((Q_BLOCKS_PER_STEP_B, KV_COLS, N_HEADS * tq), BF16),
            pltpu.VMEM((4, tq, N_HEADS * tq), F32),
        ],
        compiler_params=pltpu.CompilerParams(
            dimension_semantics=("arbitrary", "arbitrary"), vmem_limit_bytes=ATTN_VMEM_LIMIT),
        name="attn_b",
    )(rel_bias, qt, k, vt, bucket_t, sink_row)


def _back_kernel(x1_ref, oa_ref, ob_ref, g_ref, wba_ref, wbb_ref, wo_ref, n2_ref, w2i_ref, w2o_ref, nf_ref, y_ref):
    ya = jnp.dot(oa_ref[...], wba_ref[...], preferred_element_type=F32)
    yb = jnp.dot(ob_ref[...], wbb_ref[...], preferred_element_type=F32)
    g = g_ref[...].astype(F32)
    merged = _sigmoid(g[:, :D_MODEL]) * ya + _sigmoid(g[:, D_MODEL:]) * yb
    x2 = x1_ref[...] + jnp.dot(merged.astype(BF16), wo_ref[...], preferred_element_type=F32)
    x3 = _swiglu_half_step(x2, n2_ref, w2i_ref, w2o_ref)
    y_ref[...] = _rms(x3, nf_ref[...])


def _back(x1, oa, ob, g, p):
    tm = TOKEN_TILE
    n = x1.shape[0]
    tok = lambda i: (i, 0)
    return pl.pallas_call(
        _back_kernel,
        grid=(n // tm,),
        in_specs=[
            pl.BlockSpec((tm, D_MODEL), tok),
            pl.BlockSpec((tm, Q_COLS), tok),
            pl.BlockSpec((tm, Q_COLS), tok),
            pl.BlockSpec((tm, 2 * D_MODEL), tok),
            _resident((Q_COLS, D_MODEL)),
            _resident((Q_COLS, D_MODEL)),
            _resident((D_MODEL, D_MODEL)),
            _resident((1, D_MODEL)),
            _resident((D_MODEL, 2 * D_FF)),
            _resident((D_FF, D_MODEL)),
            _resident((1, D_MODEL)),
        ],
        out_specs=pl.BlockSpec((tm, D_MODEL), tok),
        out_shape=jax.ShapeDtypeStruct((n, D_MODEL), F32),
        compiler_params=pltpu.CompilerParams(
            dimension_semantics=("arbitrary",), vmem_limit_bytes=TOKEN_KERNEL_VMEM_LIMIT),
        name="back",
    )(x1, oa, ob, g, p["wba"], p["wbb"], p["wo"], p["n2"], p["w2i"], p["w2o"], p["nf"])


def _t5_bucket(rel):
    nb = N_BUCKETS // 2
    max_exact = nb // 2
    ret = jnp.where(rel > 0, nb, 0)
    n = jnp.abs(rel)
    large = max_exact + (jnp.log(jnp.maximum(n, 1).astype(F32) / max_exact)
                         / math.log(MAX_DISTANCE / max_exact) * (nb - max_exact)).astype(jnp.int32)
    large = jnp.minimum(large, nb - 1)
    return ret + jnp.where(n < max_exact, n, large)


def _rope_table(seq):
    rows = seq // GRID_W
    grid_r, grid_c = jnp.meshgrid(jnp.arange(rows, dtype=F32), jnp.arange(GRID_W, dtype=F32), indexing="ij")
    pos = jnp.stack([grid_r.reshape(-1), grid_c.reshape(-1)], axis=-1)
    inv = ROPE_THETA ** (-jnp.arange(0, AXIS_DIM, 2, dtype=F32) / AXIS_DIM)
    ang = pos[:, :, None] * inv
    cos = jnp.cos(ang)
    sin = jnp.sin(ang)
    return jnp.concatenate([cos[:, 0].T, sin[:, 0].T, cos[:, 1].T, sin[:, 1].T], axis=0)


def _trunk(x, p, bucket_t, sink_row, rel_bias, direct_ok):
    batch, seq, _ = x.shape
    cs = _rope_table(seq)
    x1, qa, ka, va, qb, kb, vb, g = _front(x.reshape(batch * seq, D_MODEL), p, cs, batch, seq)
    oa = _attn_a(qa, ka, va, direct_ok, batch, seq)
    ob = _attn_b(qb, kb, vb, rel_bias, bucket_t, sink_row, batch, seq)
    y = _back(x1, oa.reshape(batch * seq, Q_COLS), ob.reshape(batch * seq, Q_COLS), g, p)
    return y.reshape(batch, seq, D_MODEL)


def kernel(x_prompt, x_sample, norm_ffn1, w_ffn1_in, w_ffn1_out, norm_mix, w_in, q_norm_a, k_norm_a, sink_b,
           w_branch_a, w_branch_b, w_out, norm_ffn2, w_ffn2_in, w_ffn2_out, rel_bias, norm_final):
    assert norm_ffn1.shape[0] == 1, "single-layer trunk"
    wi = w_in[0]
    c = 0
    parts = {}
    for name, width in (("qa", Q_COLS), ("ka", KV_COLS), ("va", KV_COLS), ("qb", Q_COLS), ("kb", KV_COLS),
                        ("vb", KV_COLS), ("ga", D_MODEL), ("gb", D_MODEL)):
        parts[name] = wi[:, c:c + width]
        c += width
    wfm = jnp.concatenate([parts[k] for k in ("qa", "ka", "va", "qb", "vb")], axis=1).T.astype(BF16)
    wtm = jnp.concatenate([parts[k] for k in ("kb", "ga", "gb")], axis=1).astype(BF16)
    p = dict(
        n1=norm_ffn1[0].reshape(1, D_MODEL), w1i=w_ffn1_in[0].astype(BF16), w1o=w_ffn1_out[0].astype(BF16),
        nm=norm_mix[0].reshape(1, D_MODEL), wfm=wfm, wtm=wtm,
        qn=q_norm_a[0].reshape(HEAD_DIM, 1), kn=k_norm_a[0].reshape(HEAD_DIM, 1),
        wba=w_branch_a[0].astype(BF16), wbb=w_branch_b[0].astype(BF16), wo=w_out[0].astype(BF16),
        n2=norm_ffn2[0].reshape(1, D_MODEL), w2i=w_ffn2_in[0].astype(BF16), w2o=w_ffn2_out[0].astype(BF16),
        nf=norm_final.reshape(1, D_MODEL),
    )
    key = jnp.arange(3 * Q_BLOCK)[:, None] - Q_BLOCK
    qry = jnp.arange(Q_BLOCK)[None, :]
    bucket_t = _t5_bucket(key - qry).astype(jnp.int32).reshape(3, Q_BLOCK, Q_BLOCK)
    sink_row = jnp.repeat(sink_b[0].astype(F32), Q_BLOCK).reshape(1, N_HEADS * Q_BLOCK)
    rb = rel_bias.astype(F32)
    logit_bound = (1.02 * HEAD_DIM * math.log2(math.e) / math.sqrt(HEAD_DIM)
                   * jnp.max(jnp.abs(q_norm_a[0])) * jnp.max(jnp.abs(k_norm_a[0])))
    direct_ok = logit_bound <= MAX_DIRECT_LOGIT
    y_prompt = _trunk(x_prompt, p, bucket_t, sink_row, rb, direct_ok)
    y_sample = _trunk(x_sample, p, bucket_t, sink_row, rb, direct_ok)
    return (y_prompt, y_sample)
```

```python
import functools
import math

import jax
import jax.numpy as jnp
from jax import lax
from jax.experimental import pallas as pl
from jax.experimental.pallas import tpu as pltpu

D_MODEL = 1024
GRID_W = 64
HEAD_DIM = 64
N_HEADS = 8
N_KV = 2
GROUP = N_HEADS // N_KV
Q_BLOCK = 128
WINDOW = 128
N_BUCKETS = 32
MAX_DISTANCE = 128
ROPE_THETA = 10000.0
AXIS_DIM = HEAD_DIM // 2
ROT = AXIS_DIM // 2
D_FF = 2816
EPS = 1e-6
NEG_INF = -1e30
Q_COLS = N_HEADS * HEAD_DIM
KV_COLS = N_KV * HEAD_DIM

F32 = jnp.float32
BF16 = jnp.bfloat16

V7X_LANES = 128
V7X_VMEM_BYTES = 64 * 1024 * 1024
MIB = 1024 * 1024
TOKEN_KERNEL_VMEM_LIMIT = V7X_VMEM_BYTES - 8 * MIB
ATTN_VMEM_LIMIT = 40 * MIB

TOKEN_TILE = 512
KEY_TILE_A = 256
KEY_TILE_A_DIRECT = 256
Q_TILE_A = 128
Q_TILE_A_DIRECT = 128
Q_TILES_PER_STEP_A = 2
Q_CHAINS_PER_STEP_A = 2
Q_BLOCKS_PER_STEP_B = 8
VT_CHUNK = 128
SUM_ROWS = 16
DIRECT_TILES_PER_TRIP = 16
MAX_DIRECT_LOGIT = 32.0

FM_ROWS = Q_COLS + KV_COLS + KV_COLS + Q_COLS + KV_COLS
TM_COLS = KV_COLS + 2 * D_MODEL


def _resident(shape):
    nd = len(shape)
    return pl.BlockSpec(shape, lambda *_: (0,) * nd, pipeline_mode=pl.Buffered(1))


def _rms(x, g):
    ms = jnp.mean(x * x, axis=-1, keepdims=True)
    return x * lax.rsqrt(ms + EPS) * g


def _sigmoid(x):
    return 0.5 * jnp.tanh(0.5 * x) + 0.5


def _prenorm_operand(x, g):
    r = lax.rsqrt(jnp.mean(x * x, axis=-1, keepdims=True) + EPS)
    return (x * g).astype(BF16), r


def _swiglu_half_step(x, g_ref, w_in_ref, w_out_ref):
    xg, r = _prenorm_operand(x, g_ref[...])
    ab = jnp.dot(xg, w_in_ref[...], preferred_element_type=F32)
    ah = ab[:, :D_FF] * (0.5 * r)
    b = ab[:, D_FF:] * r
    act = (ah * (1.0 + jnp.tanh(ah)) * b).astype(BF16)
    return x + 0.5 * jnp.dot(act, w_out_ref[...], preferred_element_type=F32)


def _head_norm_rope_t(z, gain, cs):
    ms = jnp.mean(z * z, axis=0, keepdims=True)
    z = z * lax.rsqrt(ms + EPS) * gain
    cr, sr, cc, sc = cs[0:ROT], cs[ROT:2 * ROT], cs[2 * ROT:3 * ROT], cs[3 * ROT:4 * ROT]
    x1r, x2r = z[0:ROT], z[ROT:2 * ROT]
    x1c, x2c = z[2 * ROT:3 * ROT], z[3 * ROT:4 * ROT]
    return jnp.concatenate(
        [x1r * cr - x2r * sr, x2r * cr + x1r * sr, x1c * cc - x2c * sc, x2c * cc + x1c * sc], axis=0)


def _front_kernel(x_ref, n1_ref, w1i_ref, w1o_ref, nm_ref, wfm_ref, wtm_ref, qn_ref, kn_ref, cs_ref,
                  x1_ref, qa_ref, ka_ref, va_ref, qb_ref, kb_ref, vb_ref, g_ref):
    tm = x_ref.shape[0]
    x1 = _swiglu_half_step(x_ref[...], n1_ref, w1i_ref, w1o_ref)
    x1_ref[...] = x1
    h, r = _prenorm_operand(x1, nm_ref[...])
    r_t = jnp.broadcast_to(r, (tm, V7X_LANES)).T[:1]
    pt = lax.dot_general(wfm_ref[...], h, (((1,), (1,)), ((), ())), preferred_element_type=F32) * r_t
    ptm = jnp.dot(h, wtm_ref[...], preferred_element_type=F32) * r
    cs = cs_ref[...]
    qn = qn_ref[...]
    kn = kn_ref[...]
    scale = math.log2(math.e) / math.sqrt(HEAD_DIM)
    for hd in range(N_HEADS):
        z = pt[hd * HEAD_DIM:(hd + 1) * HEAD_DIM]
        qa_ref[0, hd * HEAD_DIM:(hd + 1) * HEAD_DIM, :] = (_head_norm_rope_t(z, qn, cs) * scale).astype(BF16)
    ka_t = jnp.concatenate(
        [_head_norm_rope_t(pt[Q_COLS + kv * HEAD_DIM:Q_COLS + (kv + 1) * HEAD_DIM], kn, cs) for kv in range(N_KV)],
        axis=0)
    ka_ref[0] = ka_t.T.astype(BF16)
    off = Q_COLS + KV_COLS
    va = pt[off:off + KV_COLS].astype(BF16)
    off += KV_COLS
    qb_ref[0] = (pt[off:off + Q_COLS] * scale).astype(BF16)
    off += Q_COLS
    vb = pt[off:off + KV_COLS].astype(BF16)
    for c in range(tm // VT_CHUNK):
        va_ref[0, c] = va[:, c * VT_CHUNK:(c + 1) * VT_CHUNK]
        vb_ref[0, c] = vb[:, c * VT_CHUNK:(c + 1) * VT_CHUNK]
    kb_ref[0] = ptm[:, :KV_COLS].astype(BF16)
    g_ref[...] = ptm[:, KV_COLS:].astype(BF16)


def _front(x, p, cs, batch, seq):
    tm = TOKEN_TILE
    n = batch * seq
    nt = seq // tm
    grid = (n // tm,)
    tok = lambda i: (i, 0)
    bt3 = lambda i: (i // nt, 0, i % nt)
    out_shape = (
        jax.ShapeDtypeStruct((n, D_MODEL), F32),
        jax.ShapeDtypeStruct((batch, Q_COLS, seq), BF16),
        jax.ShapeDtypeStruct((batch, seq, KV_COLS), BF16),
        jax.ShapeDtypeStruct((batch, seq // VT_CHUNK, KV_COLS, VT_CHUNK), BF16),
        jax.ShapeDtypeStruct((batch, Q_COLS, seq), BF16),
        jax.ShapeDtypeStruct((batch, seq, KV_COLS), BF16),
        jax.ShapeDtypeStruct((batch, seq // VT_CHUNK, KV_COLS, VT_CHUNK), BF16),
        jax.ShapeDtypeStruct((n, 2 * D_MODEL), BF16),
    )
    vt_spec = pl.BlockSpec((1, tm // VT_CHUNK, KV_COLS, VT_CHUNK), lambda i: (i // nt, i % nt, 0, 0))
    out_specs = (
        pl.BlockSpec((tm, D_MODEL), tok),
        pl.BlockSpec((1, Q_COLS, tm), bt3),
        pl.BlockSpec((1, tm, KV_COLS), lambda i: (i // nt, i % nt, 0)),
        vt_spec,
        pl.BlockSpec((1, Q_COLS, tm), bt3),
        pl.BlockSpec((1, tm, KV_COLS), lambda i: (i // nt, i % nt, 0)),
        vt_spec,
        pl.BlockSpec((tm, 2 * D_MODEL), tok),
    )
    in_specs = [
        pl.BlockSpec((tm, D_MODEL), tok),
        _resident((1, D_MODEL)),
        _resident((D_MODEL, 2 * D_FF)),
        _resident((D_FF, D_MODEL)),
        _resident((1, D_MODEL)),
        _resident((FM_ROWS, D_MODEL)),
        _resident((D_MODEL, TM_COLS)),
        _resident((HEAD_DIM, 1)),
        _resident((HEAD_DIM, 1)),
        pl.BlockSpec((4 * ROT, tm), lambda i: (0, i % nt)),
    ]
    return pl.pallas_call(
        _front_kernel,
        grid=grid,
        in_specs=in_specs,
        out_specs=out_specs,
        out_shape=out_shape,
        compiler_params=pltpu.CompilerParams(
            dimension_semantics=("arbitrary",), vmem_limit_bytes=TOKEN_KERNEL_VMEM_LIMIT),
        name="front",
    )(x, p["n1"], p["w1i"], p["w1o"], p["nm"], p["wfm"], p["wtm"], p["qn"], p["kn"], cs)


def _build_q_weights(q, wq_ref, tq):
    zeros = jnp.zeros((HEAD_DIM, tq), BF16)
    for hd in range(N_HEADS):
        qh = q[hd * HEAD_DIM:(hd + 1) * HEAD_DIM]
        blk = jnp.concatenate([qh, zeros], axis=0) if hd < GROUP else jnp.concatenate([zeros, qh], axis=0)
        wq_ref[:, hd * tq:(hd + 1) * tq] = blk


def _attn_a_online_kernel(q_ref, k_ref, v_ref, o_ref, wq_ref, s0_ref, s1_ref, t0_ref, t1_ref, m_ref, l_ref,
                          acc_ref, *, seq):
    tq = q_ref.shape[2]
    half = GROUP * tq
    tk = KEY_TILE_A
    cpt = tk // VT_CHUNK
    nk = seq // tk
    _build_q_weights(q_ref[0], wq_ref, tq)
    m_ref[...] = jnp.full(m_ref.shape, NEG_INF, F32)
    l_ref[...] = jnp.zeros(l_ref.shape, F32)
    acc_ref[...] = jnp.zeros(acc_ref.shape, F32)

    def scores(j, s_ref, t_ref):
        kt = k_ref[0, pl.ds(pl.multiple_of(j * tk, tk), tk), :]
        s = jnp.dot(kt, wq_ref[...], preferred_element_type=F32)
        s_ref[...] = s
        t_ref[...] = jnp.max(s, axis=0, keepdims=True)

    def softmax_pv(j, s_ref, t_ref):
        m_old = m_ref[...]
        m_new = jnp.maximum(m_old, t_ref[...])
        alpha = jnp.exp2(m_old - m_new)
        p = jnp.exp2(s_ref[...] - m_new)
        l_ref[...] = alpha * l_ref[...] + jnp.sum(p, axis=0, keepdims=True)
        m_ref[...] = m_new
        pb = p.astype(BF16)
        vt = jnp.concatenate([v_ref[0, j * cpt + c] for c in range(cpt)], axis=1)
        for kv in range(N_KV):
            cols = slice(kv * half, (kv + 1) * half)
            pv = jnp.dot(vt[kv * HEAD_DIM:(kv + 1) * HEAD_DIM], pb[:, cols], preferred_element_type=F32)
            acc_ref[kv] = acc_ref[kv] * alpha[:, cols] + pv

    scores(0, s0_ref, t0_ref)

    def body(jj, carry):
        j = 2 * jj
        scores(j + 1, s1_ref, t1_ref)
        softmax_pv(j, s0_ref, t0_ref)
        scores(jnp.minimum(j + 2, nk - 1), s0_ref, t0_ref)
        softmax_pv(j + 1, s1_ref, t1_ref)
        return carry

    lax.fori_loop(0, nk // 2, body, 0)
    inv_l = 1.0 / l_ref[...]
    rows = []
    for hd in range(N_HEADS):
        kv, g = divmod(hd, GROUP)
        cols = slice(g * tq, (g + 1) * tq)
        rows.append(acc_ref[kv][:, cols] * inv_l[:, hd * tq:(hd + 1) * tq])
    o_ref[0] = jnp.concatenate(rows, axis=0).T.astype(BF16)


def _attn_a_online(qt, k, vt, batch, seq):
    tq = Q_TILE_A
    assert (seq // KEY_TILE_A) % 2 == 0
    kernel = functools.partial(_attn_a_online_kernel, seq=seq)
    row = pltpu.VMEM((1, N_HEADS * tq), F32)
    tile = pltpu.VMEM((KEY_TILE_A, N_HEADS * tq), F32)
    return pl.pallas_call(
        kernel,
        grid=(batch, seq // tq),
        in_specs=[
            pl.BlockSpec((1, Q_COLS, tq), lambda b, i: (b, 0, i)),
            pl.BlockSpec((1, seq, KV_COLS), lambda b, i: (b, 0, 0)),
            pl.BlockSpec((1, seq // VT_CHUNK, KV_COLS, VT_CHUNK), lambda b, i: (b, 0, 0, 0)),
        ],
        out_specs=pl.BlockSpec((1, tq, Q_COLS), lambda b, i: (b, i, 0)),
        out_shape=jax.ShapeDtypeStruct((batch, seq, Q_COLS), BF16),
        scratch_shapes=[
            pltpu.VMEM((KV_COLS, N_HEADS * tq), BF16),
            tile, tile,
            row, row,
            row, row,
            pltpu.VMEM((N_KV, HEAD_DIM, GROUP * tq), F32),
        ],
        compiler_params=pltpu.CompilerParams(
            dimension_semantics=("arbitrary", "arbitrary"), vmem_limit_bytes=ATTN_VMEM_LIMIT),
        name="attn_a_online",
    )(qt, k, vt)


def _attn_a_direct_kernel(q_ref, k_ref, v_ref, o_ref, wq_ref, p0_ref, p1_ref, acc_ref, *, seq, per_trip):
    tq = Q_TILE_A_DIRECT
    half = GROUP * tq
    tk = KEY_TILE_A_DIRECT
    cpt = tk // VT_CHUNK
    nk = seq // tk
    p_refs = (p0_ref, p1_ref)

    def probs(sub, j, p_ref):
        kt = k_ref[0, pl.ds(pl.multiple_of(j * tk, tk), tk), :]
        s = jnp.dot(kt, wq_ref[sub], preferred_element_type=F32)
        p_ref[...] = jnp.exp2(s).astype(BF16)

    def pv(sub, j, p_ref):
        vt = jnp.concatenate([v_ref[0, j * cpt + c] for c in range(cpt)], axis=1)
        ones = jnp.ones((SUM_ROWS, tk), BF16)
        for kv in range(N_KV):
            cols = slice(kv * half, (kv + 1) * half)
            lhs = jnp.concatenate([vt[kv * HEAD_DIM:(kv + 1) * HEAD_DIM], ones], axis=0)
            acc_ref[sub, kv] += jnp.dot(lhs, p_ref[:, cols], preferred_element_type=F32)

    def start(base, sub):
        q = q_ref[0, :, pl.ds(pl.multiple_of(base + sub * tq, tq), tq)]
        _build_q_weights(q, wq_ref.at[sub], tq)
        acc_ref[sub] = jnp.zeros(acc_ref.shape[1:], F32)
        probs(sub, 0, p0_ref)

    def finish(base, sub):
        rows = []
        for hd in range(N_HEADS):
            kv, g = divmod(hd, GROUP)
            cols = slice(g * tq, (g + 1) * tq)
            acc = acc_ref[sub, kv]
            rows.append(acc[:HEAD_DIM, cols] * (1.0 / acc[HEAD_DIM:HEAD_DIM + 1, cols]))
        out = jnp.concatenate(rows, axis=0).T.astype(BF16)
        o_ref[0, pl.ds(pl.multiple_of(base + sub * tq, tq), tq), :] = out

    def chain(cc, carry):
        base = cc * (Q_TILES_PER_STEP_A * tq)
        start(base, 0)
        for sub in range(Q_TILES_PER_STEP_A):

            def body(jj, c, sub=sub):
                for u in range(per_trip):
                    probs(sub, per_trip * jj + u + 1, p_refs[(u + 1) % 2])
                    pv(sub, per_trip * jj + u, p_refs[u % 2])
                return c

            lax.fori_loop(0, nk // per_trip - 1, body, 0)
            j0 = nk - per_trip
            for u in range(per_trip):
                if u < per_trip - 1:
                    probs(sub, j0 + u + 1, p_refs[(u + 1) % 2])
                elif sub + 1 < Q_TILES_PER_STEP_A:
                    start(base, sub + 1)
                pv(sub, j0 + u, p_refs[u % 2])
            finish(base, sub)
        return carry

    lax.fori_loop(0, Q_CHAINS_PER_STEP_A, chain, 0)


def _attn_a_direct(qt, k, vt, batch, seq):
    tq = Q_TILE_A_DIRECT
    tqs = Q_CHAINS_PER_STEP_A * Q_TILES_PER_STEP_A * tq
    nk = seq // KEY_TILE_A_DIRECT
    per_trip = min(DIRECT_TILES_PER_TRIP, nk)
    assert per_trip % 2 == 0 and nk % per_trip == 0 and seq % tqs == 0
    kernel = functools.partial(_attn_a_direct_kernel, seq=seq, per_trip=per_trip)
    p_tile = pltpu.VMEM((KEY_TILE_A_DIRECT, N_HEADS * tq), BF16)
    return pl.pallas_call(
        kernel,
        grid=(batch, seq // tqs),
        in_specs=[
            pl.BlockSpec((1, Q_COLS, tqs), lambda b, i: (b, 0, i)),
            pl.BlockSpec((1, seq, KV_COLS), lambda b, i: (b, 0, 0)),
            pl.BlockSpec((1, seq // VT_CHUNK, KV_COLS, VT_CHUNK), lambda b, i: (b, 0, 0, 0)),
        ],
        out_specs=pl.BlockSpec((1, tqs, Q_COLS), lambda b, i: (b, i, 0)),
        out_shape=jax.ShapeDtypeStruct((batch, seq, Q_COLS), BF16),
        scratch_shapes=[
            pltpu.VMEM((Q_TILES_PER_STEP_A, KV_COLS, N_HEADS * tq), BF16),
            p_tile, p_tile,
            pltpu.VMEM((Q_TILES_PER_STEP_A, N_KV, HEAD_DIM + SUM_ROWS, GROUP * tq), F32),
        ],
        compiler_params=pltpu.CompilerParams(
            dimension_semantics=("arbitrary", "arbitrary"), vmem_limit_bytes=ATTN_VMEM_LIMIT),
        name="attn_a_direct",
    )(qt, k, vt)


def _attn_a(qt, k, vt, direct_ok, batch, seq):
    return lax.cond(direct_ok,
                    lambda: _attn_a_direct(qt, k, vt, batch, seq),
                    lambda: _attn_a_online(qt, k, vt, batch, seq))


def _attn_b_kernel(rb_ref, q_ref, k_ref, v_ref, bucket_ref, sink_ref, o_ref, wq_ref, bias_ref, *, seq):
    step = pl.program_id(1)

    @pl.when((pl.program_id(0) == 0) & (step == 0))
    def _():
        _window_bias_tables(rb_ref, bucket_ref, bias_ref)

    sink = sink_ref[...] * math.log2(math.e)
    logits = []
    for sub in range(Q_BLOCKS_PER_STEP_B):
        wq = wq_ref.at[sub]
        _build_q_weights(q_ref[0, :, sub * Q_BLOCK:(sub + 1) * Q_BLOCK], wq, Q_BLOCK)
        logits.append(_window_logits(step * Q_BLOCKS_PER_STEP_B + sub, wq, k_ref, bias_ref, seq))
    for sub, (s, idxs) in enumerate(logits):
        o_ref[0, sub * Q_BLOCK:(sub + 1) * Q_BLOCK, :] = _window_softmax_pv(s, idxs, sink, v_ref)


def _window_bias_tables(rb_ref, bucket_ref, bias_ref):
    tq = Q_BLOCK
    log2e = math.log2(math.e)
    for jb in range(3):
        bucket = bucket_ref[jb]
        key = lax.broadcasted_iota(jnp.int32, (tq, tq), 0)
        qry = lax.broadcasted_iota(jnp.int32, (tq, tq), 1)
        rel = (jb - 1) * tq + key - qry
        inside = jnp.abs(rel) <= WINDOW
        for hd in range(N_HEADS):
            tbl = jnp.zeros((tq, tq), F32)
            for b in range(N_BUCKETS):
                tbl = jnp.where(bucket == b, rb_ref[b, hd], tbl)
            bias_ref[jb, :, hd * tq:(hd + 1) * tq] = jnp.where(inside, tbl * log2e, NEG_INF)
    bias_ref[3] = jnp.full(bias_ref.shape[1:], NEG_INF, F32)


def _window_logits(i, wq, k_ref, bias_ref, seq):
    tq = Q_BLOCK
    nb = seq // tq
    parts = []
    idxs = []
    for jb in range(3):
        blk = i - 1 + jb
        valid = (blk >= 0) & (blk < nb)
        idx = jnp.clip(blk, 0, nb - 1)
        kt = k_ref[0, pl.ds(pl.multiple_of(idx * tq, tq), tq), :]
        tbl = bias_ref[jnp.where(valid, jb, 3)]
        parts.append(jnp.dot(kt, wq[...], preferred_element_type=F32) + tbl)
        idxs.append(idx)
    return jnp.concatenate(parts, axis=0), idxs


def _window_softmax_pv(s, idxs, sink, v_ref):
    tq = Q_BLOCK
    half = GROUP * tq
    ones = jnp.ones((SUM_ROWS, tq), BF16)
    m = jnp.maximum(jnp.max(s, axis=0, keepdims=True), sink)
    pb = jnp.exp2(s - m).astype(BF16)
    p_sink = jnp.exp2(sink - m)
    rows = []
    for kv in range(N_KV):
        cols = slice(kv * half, (kv + 1) * half)
        acc = jnp.zeros((HEAD_DIM + SUM_ROWS, half), F32)
        for jb in range(3):
            lhs = jnp.concatenate([v_ref[0, idxs[jb]][kv * HEAD_DIM:(kv + 1) * HEAD_DIM], ones], axis=0)
            acc = acc + jnp.dot(lhs, pb[jb * tq:(jb + 1) * tq, cols], preferred_element_type=F32)
        out = acc[:HEAD_DIM] * (1.0 / (acc[HEAD_DIM:HEAD_DIM + 1] + p_sink[:, cols]))
        for g in range(GROUP):
            rows.append(out[:, g * tq:(g + 1) * tq])
    return jnp.concatenate(rows, axis=0).T.astype(BF16)


def _attn_b(qt, k, vt, rel_bias, bucket_t, sink_row, batch, seq):
    tq = Q_BLOCK
    tqs = Q_BLOCKS_PER_STEP_B * tq
    assert seq % tqs == 0
    kernel = functools.partial(_attn_b_kernel, seq=seq)
    return pl.pallas_call(
        kernel,
        grid=(batch, seq // tqs),
        in_specs=[
            pl.BlockSpec(memory_space=pltpu.SMEM),
            pl.BlockSpec((1, Q_COLS, tqs), lambda b, i: (b, 0, i)),
            pl.BlockSpec((1, seq, KV_COLS), lambda b, i: (b, 0, 0)),
            pl.BlockSpec((1, seq // VT_CHUNK, KV_COLS, VT_CHUNK), lambda b, i: (b, 0, 0, 0)),
            pl.BlockSpec((3, tq, tq), lambda b, i: (0, 0, 0)),
            pl.BlockSpec((1, N_HEADS * tq), lambda b, i: (0, 0)),
        ],
        out_specs=pl.BlockSpec((1, tqs, Q_COLS), lambda b, i: (b, i, 0)),
        out_shape=jax.ShapeDtypeStruct((batch, seq, Q_COLS), BF16),
        scratch_shapes=[
            pltpu.VMEM((Q_BLOCKS_PER_STEP_B, KV_COLS, N_HEADS * tq), BF16),
            pltpu.VMEM((4, tq, N_HEADS * tq), F32),
        ],
        compiler_params=pltpu.CompilerParams(
            dimension_semantics=("arbitrary", "arbitrary"), vmem_limit_bytes=ATTN_VMEM_LIMIT),
        name="attn_b",
    )(rel_bias, qt, k, vt, bucket_t, sink_row)


def _back_kernel(x1_ref, oa_ref, ob_ref, g_ref, wba_ref, wbb_ref, wo_ref, n2_ref, w2i_ref, w2o_ref, nf_ref, y_ref):
    ya = jnp.dot(oa_ref[...], wba_ref[...], preferred_element_type=F32)
    yb = jnp.dot(ob_ref[...], wbb_ref[...], preferred_element_type=F32)
    g = g_ref[...].astype(F32)
    merged = _sigmoid(g[:, :D_MODEL]) * ya + _sigmoid(g[:, D_MODEL:]) * yb
    x2 = x1_ref[...] + jnp.dot(merged.astype(BF16), wo_ref[...], preferred_element_type=F32)
    x3 = _swiglu_half_step(x2, n2_ref, w2i_ref, w2o_ref)
    y_ref[...] = _rms(x3, nf_ref[...])


def _back(x1, oa, ob, g, p):
    tm = TOKEN_TILE
    n = x1.shape[0]
    tok = lambda i: (i, 0)
    return pl.pallas_call(
        _back_kernel,
        grid=(n // tm,),
        in_specs=[
            pl.BlockSpec((tm, D_MODEL), tok),
            pl.BlockSpec((tm, Q_COLS), tok),
            pl.BlockSpec((tm, Q_COLS), tok),
            pl.BlockSpec((tm, 2 * D_MODEL), tok),
            _resident((Q_COLS, D_MODEL)),
            _resident((Q_COLS, D_MODEL)),
            _resident((D_MODEL, D_MODEL)),
            _resident((1, D_MODEL)),
            _resident((D_MODEL, 2 * D_FF)),
            _resident((D_FF, D_MODEL)),
            _resident((1, D_MODEL)),
        ],
        out_specs=pl.BlockSpec((tm, D_MODEL), tok),
        out_shape=jax.ShapeDtypeStruct((n, D_MODEL), F32),
        compiler_params=pltpu.CompilerParams(
            dimension_semantics=("arbitrary",), vmem_limit_bytes=TOKEN_KERNEL_VMEM_LIMIT),
        name="back",
    )(x1, oa, ob, g, p["wba"], p["wbb"], p["wo"], p["n2"], p["w2i"], p["w2o"], p["nf"])


def _t5_bucket(rel):
    nb = N_BUCKETS // 2
    max_exact = nb // 2
    ret = jnp.where(rel > 0, nb, 0)
    n = jnp.abs(rel)
    large = max_exact + (jnp.log(jnp.maximum(n, 1).astype(F32) / max_exact)
                         / math.log(MAX_DISTANCE / max_exact) * (nb - max_exact)).astype(jnp.int32)
    large = jnp.minimum(large, nb - 1)
    return ret + jnp.where(n < max_exact, n, large)


def _rope_table(seq):
    rows = seq // GRID_W
    grid_r, grid_c = jnp.meshgrid(jnp.arange(rows, dtype=F32), jnp.arange(GRID_W, dtype=F32), indexing="ij")
    pos = jnp.stack([grid_r.reshape(-1), grid_c.reshape(-1)], axis=-1)
    inv = ROPE_THETA ** (-jnp.arange(0, AXIS_DIM, 2, dtype=F32) / AXIS_DIM)
    ang = pos[:, :, None] * inv
    cos = jnp.cos(ang)
    sin = jnp.sin(ang)
    return jnp.concatenate([cos[:, 0].T, sin[:, 0].T, cos[:, 1].T, sin[:, 1].T], axis=0)


def _trunk(x, p, bucket_t, sink_row, rel_bias, direct_ok):
    batch, seq, _ = x.shape
    cs = _rope_table(seq)
    x1, qa, ka, va, qb, kb, vb, g = _front(x.reshape(batch * seq, D_MODEL), p, cs, batch, seq)
    oa = _attn_a(qa, ka, va, direct_ok, batch, seq)
    ob = _attn_b(qb, kb, vb, rel_bias, bucket_t, sink_row, batch, seq)
    y = _back(x1, oa.reshape(batch * seq, Q_COLS), ob.reshape(batch * seq, Q_COLS), g, p)
    return y.reshape(batch, seq, D_MODEL)


def kernel(x_prompt, x_sample, norm_ffn1, w_ffn1_in, w_ffn1_out, norm_mix, w_in, q_norm_a, k_norm_a, sink_b,
           w_branch_a, w_branch_b, w_out, norm_ffn2, w_ffn2_in, w_ffn2_out, rel_bias, norm_final):
    assert norm_ffn1.shape[0] == 1, "single-layer trunk"
    wi = w_in[0]
    c = 0
    parts = {}
    for name, width in (("qa", Q_COLS), ("ka", KV_COLS), ("va", KV_COLS), ("qb", Q_COLS), ("kb", KV_COLS),
                        ("vb", KV_COLS), ("ga", D_MODEL), ("gb", D_MODEL)):
        parts[name] = wi[:, c:c + width]
        c += width
    wfm = jnp.concatenate([parts[k] for k in ("qa", "ka", "va", "qb", "vb")], axis=1).T.astype(BF16)
    wtm = jnp.concatenate([parts[k] for k in ("kb", "ga", "gb")], axis=1).astype(BF16)
    p = dict(
        n1=norm_ffn1[0].reshape(1, D_MODEL), w1i=w_ffn1_in[0].astype(BF16), w1o=w_ffn1_out[0].astype(BF16),
        nm=norm_mix[0].reshape(1, D_MODEL), wfm=wfm, wtm=wtm,
        qn=q_norm_a[0].reshape(HEAD_DIM, 1), kn=k_norm_a[0].reshape(HEAD_DIM, 1),
        wba=w_branch_a[0].astype(BF16), wbb=w_branch_b[0].astype(BF16), wo=w_out[0].astype(BF16),
        n2=norm_ffn2[0].reshape(1, D_MODEL), w2i=w_ffn2_in[0].astype(BF16), w2o=w_ffn2_out[0].astype(BF16),
        nf=norm_final.reshape(1, D_MODEL),
    )
    key = jnp.arange(3 * Q_BLOCK)[:, None] - Q_BLOCK
    qry = jnp.arange(Q_BLOCK)[None, :]
    bucket_t = _t5_bucket(key - qry).astype(jnp.int32).reshape(3, Q_BLOCK, Q_BLOCK)
    sink_row = jnp.repeat(sink_b[0].astype(F32), Q_BLOCK).reshape(1, N_HEADS * Q_BLOCK)
    rb = rel_bias.astype(F32)
    logit_bound = (1.02 * HEAD_DIM * math.log2(math.e) / math.sqrt(HEAD_DIM)
                   * jnp.max(jnp.abs(q_norm_a[0])) * jnp.max(jnp.abs(k_norm_a[0])))
    direct_ok = logit_bound <= MAX_DIRECT_LOGIT
    y_prompt = _trunk(x_prompt, p, bucket_t, sink_row, rb, direct_ok)
    y_sample = _trunk(x_sample, p, bucket_t, sink_row, rb, direct_ok)
    return (y_prompt, y_sample)
```
